```python
import math
import jax, jax.numpy as jnp
from jax import lax
import numpy as np

D_MODEL = 1024
BATCH = 16
SEQ = 4096
DEPTH = 1

MIX_WIDTH = D_MODEL
ATTN_WIDTH = MIX_WIDTH // 2
CONV_WIDTH = MIX_WIDTH - ATTN_WIDTH
HEAD_DIM = 64
N_HEADS = ATTN_WIDTH // HEAD_DIM
IN_COLS = 3 * ATTN_WIDTH + 2 * CONV_WIDTH
CONV_K = 31
BLOCK = 256
TOP_K = 3
Q_CHUNK = 32
D_FF = 2816
ALIBI_MAX = 8.0
EPS = 1e-6
NEG = -1e30

kernel_name = "hybrid_conv_moba_macaron"


def _rmsnorm(x, g):
    xf = x.astype(jnp.float32)
    y = xf * lax.rsqrt(jnp.mean(xf * xf, axis=-1, keepdims=True) + EPS)
    return (y * g.astype(jnp.float32)).astype(x.dtype)


def _layernorm(x, g, b):
    xf = x.astype(jnp.float32)
    mu = jnp.mean(xf, axis=-1, keepdims=True)
    var = jnp.mean(jnp.square(xf - mu), axis=-1, keepdims=True)
    y = (xf - mu) * lax.rsqrt(var + EPS)
    return (y * g.astype(jnp.float32) + b.astype(jnp.float32)).astype(x.dtype)


def _swiglu(h, w1, w3, w2):
    return (jax.nn.silu(h @ w1) * (h @ w3)) @ w2


def _alibi_slopes(n_heads):
    return jnp.exp2(-ALIBI_MAX * (jnp.arange(n_heads, dtype=jnp.float32) + 1.0) / n_heads)


def _conv_module(a, g, dw_w, dw_b, ln_g, ln_b):
    u = a * jax.nn.sigmoid(g)
    c = u.shape[-1]
    rhs = dw_w.astype(u.dtype)[:, None, :]
    y = lax.conv_general_dilated(u, rhs, window_strides=(1,),
                                 padding=[(CONV_K - 1, 0)],
                                 dimension_numbers=("NWC", "WIO", "NWC"),
                                 feature_group_count=c)
    y = y + dw_b.astype(u.dtype)
    y = _layernorm(y, ln_g, ln_b)
    return jax.nn.silu(y)


def _moba_attention(q, k, v, slopes):
    B, H, S, hd = q.shape
    nb = -(-S // BLOCK)
    pad = nb * BLOCK - S
    kp = jnp.pad(k, ((0, 0), (0, 0), (0, pad), (0, 0)))
    vp = jnp.pad(v, ((0, 0), (0, 0), (0, pad), (0, 0)))
    kb = kp.reshape(B, H, nb, BLOCK, hd)
    vb = vp.reshape(B, H, nb, BLOCK, hd)
    kmean = jnp.mean(kb.astype(jnp.float32), axis=3)

    pos = jnp.arange(S)
    qblk = pos // BLOCK
    gate = jnp.einsum("bhsd,bhnd->bhsn", q.astype(jnp.float32), kmean)
    fully_past = jnp.arange(nb)[None, :] < qblk[:, None]
    gate = jnp.where(fully_past, gate, NEG)
    kk = min(TOP_K, nb)
    _, sel = lax.top_k(gate, kk)
    sel_valid = jnp.arange(kk)[None, :] < qblk[:, None]

    scale = hd ** -0.5
    bi = jnp.arange(B)[:, None, None]
    hi = jnp.arange(H)[None, :, None]
    blk_off = jnp.arange(BLOCK)

    def chunk(ci):
        start = ci * Q_CHUNK
        qc = lax.dynamic_slice_in_dim(q, start, Q_CHUNK, axis=2)
        selc = lax.dynamic_slice_in_dim(sel, start, Q_CHUNK, axis=2)
        validc = lax.dynamic_slice_in_dim(sel_valid, start, Q_CHUNK, axis=0)
        qpos = start + jnp.arange(Q_CHUNK)
        flat = selc.reshape(B, H, Q_CHUNK * kk)
        kg = kb[bi, hi, flat].reshape(B, H, Q_CHUNK, kk, BLOCK, hd)
        vg = vb[bi, hi, flat].reshape(B, H, Q_CHUNK, kk, BLOCK, hd)
        kpos_sel = selc[..., None] * BLOCK + blk_off
        dist_p = (qpos[:, None, None] - kpos_sel).astype(jnp.float32)
        lp = (jnp.einsum("bhcd,bhckld->bhckl", qc, kg).astype(jnp.float32) * scale
              - slopes[:, None, None, None] * jnp.abs(dist_p))
        lp = jnp.where(validc[None, None, :, :, None], lp, NEG)
        own_start = (start // BLOCK) * BLOCK
        ko = lax.dynamic_slice_in_dim(kp, own_start, BLOCK, axis=2)
        vo = lax.dynamic_slice_in_dim(vp, own_start, BLOCK, axis=2)
        kpos_own = own_start + blk_off
        dist_o = (qpos[:, None] - kpos_own[None, :]).astype(jnp.float32)
        lo = (jnp.einsum("bhcd,bhld->bhcl", qc, ko).astype(jnp.float32) * scale
              - slopes[:, None, None] * jnp.abs(dist_o))
        lo = jnp.where(kpos_own[None, :] <= qpos[:, None], lo, NEG)
        logits = jnp.concatenate([lp.reshape(B, H, Q_CHUNK, kk * BLOCK), lo], axis=-1)
        p = jax.nn.softmax(logits, axis=-1).astype(v.dtype)
        pp = p[..., :kk * BLOCK].reshape(B, H, Q_CHUNK, kk, BLOCK)
        po = p[..., kk * BLOCK:]
        return (jnp.einsum("bhckl,bhckld->bhcd", pp, vg)
                + jnp.einsum("bhcl,bhld->bhcd", po, vo))

    outs = lax.map(chunk, jnp.arange(S // Q_CHUNK))
    return outs.transpose(1, 2, 0, 3, 4).reshape(B, H, S, hd)


def setup_inputs(seed: int = 0) -> dict:
    key = jax.random.key(seed)
    ks = jax.random.split(key, 20)
    f32 = jnp.float32

    def nrm(k, shape, fan_in):
        return jax.random.normal(k, shape, f32) * fan_in ** -0.5

    def gain(k, shape):
        return 1.0 + 0.02 * jax.random.normal(k, shape, f32)

    L = DEPTH
    return {
        "x": jax.random.normal(ks[0], (BATCH, SEQ, D_MODEL), f32),
        "ffn1_norm": gain(ks[1], (L, D_MODEL)),
        "ffn1_w1": nrm(ks[2], (L, D_MODEL, D_FF), D_MODEL),
        "ffn1_w3": nrm(ks[3], (L, D_MODEL, D_FF), D_MODEL),
        "ffn1_w2": nrm(ks[4], (L, D_FF, D_MODEL), D_FF),
        "mix_norm": gain(ks[5], (L, D_MODEL)),
        "w_in": nrm(ks[6], (L, D_MODEL, IN_COLS), D_MODEL),
        "q_norm": gain(ks[7], (L, HEAD_DIM)),
        "k_norm": gain(ks[8], (L, HEAD_DIM)),
        "conv_dw_w": nrm(ks[9], (L, CONV_K, CONV_WIDTH), CONV_K),
        "conv_dw_b": 0.02 * jax.random.normal(ks[10], (L, CONV_WIDTH), f32),
        "conv_ln_g": gain(ks[11], (L, CONV_WIDTH)),
        "conv_ln_b": 0.02 * jax.random.normal(ks[12], (L, CONV_WIDTH), f32),
        "w_out": nrm(ks[13], (L, MIX_WIDTH, D_MODEL), MIX_WIDTH),
        "ffn2_norm": gain(ks[14], (L, D_MODEL)),
        "ffn2_w1": nrm(ks[15], (L, D_MODEL, D_FF), D_MODEL),
        "ffn2_w3": nrm(ks[16], (L, D_MODEL, D_FF), D_MODEL),
        "ffn2_w2": nrm(ks[17], (L, D_FF, D_MODEL), D_FF),
    }


def reference(x, ffn1_norm, ffn1_w1, ffn1_w3, ffn1_w2, mix_norm, w_in, q_norm,
              k_norm, conv_dw_w, conv_dw_b, conv_ln_g, conv_ln_b, w_out,
              ffn2_norm, ffn2_w1, ffn2_w3, ffn2_w2):
    B, S, _ = x.shape
    slopes = _alibi_slopes(N_HEADS)
    A = ATTN_WIDTH
    for l in range(DEPTH):
        x = x + 0.5 * _swiglu(_rmsnorm(x, ffn1_norm[l]), ffn1_w1[l], ffn1_w3[l], ffn1_w2[l])

        h = _rmsnorm(x, mix_norm[l])
        z = h @ w_in[l]
        zq, zk, zv = z[..., :A], z[..., A:2 * A], z[..., 2 * A:3 * A]
        za = z[..., 3 * A:3 * A + CONV_WIDTH]
        zg = z[..., 3 * A + CONV_WIDTH:]

        def heads(t):
            return t.reshape(B, S, N_HEADS, HEAD_DIM).transpose(0, 2, 1, 3)

        q = _rmsnorm(heads(zq), q_norm[l])
        k = _rmsnorm(heads(zk), k_norm[l])
        v = heads(zv)
        attn = _moba_attention(q, k, v, slopes)
        attn = attn.transpose(0, 2, 1, 3).reshape(B, S, A)

        conv = _conv_module(za, zg, conv_dw_w[l], conv_dw_b[l],
                            conv_ln_g[l], conv_ln_b[l])

        mixed = jnp.concatenate([attn, conv], axis=-1)
        x = x + mixed @ w_out[l]

        x = x + 0.5 * _swiglu(_rmsnorm(x, ffn2_norm[l]), ffn2_w1[l], ffn2_w3[l], ffn2_w2[l])
    return x
```

```python
import functools

import numpy as np
import jax
import jax.numpy as jnp
from jax import lax
from jax.experimental import pallas as pl
from jax.experimental.pallas import tpu as pltpu

D_MODEL = 1024
ATTN_WIDTH = 512
CONV_WIDTH = 512
HEAD_DIM = 64
N_HEADS = 8
N_PAIRS = N_HEADS // 2
IN_COLS = 3 * ATTN_WIDTH + 2 * CONV_WIDTH
CONV_K = 31
BLOCK = 256
TOP_K = 3
D_FF = 2816
ALIBI_MAX = 8.0
EPS = 1e-6
MASK_BIAS = -float(2 ** 30)
NEG = -1e30

LANES = 128
FF_CHUNK = 256
FFN_ROWS = 512
PROJ_ROWS = 512
CONV_ROWS = 256
CONV_HALO = 32
VMEM_LIMIT = 56 * 1024 * 1024

_BF16 = jnp.bfloat16
_F32 = jnp.float32


def _dot(a, b):
    return jnp.dot(a, b, preferred_element_type=_F32)


def _dot_nt(a, b):
    return lax.dot_general(a, b, (((1,), (1,)), ((), ())), preferred_element_type=_F32)


def _split_bf16(x):
    hi = x.astype(_BF16)
    lo = (x - hi.astype(_F32)).astype(_BF16)
    return hi, lo


def _resident(shape):
    zeros = (0,) * len(shape)
    return pl.BlockSpec(shape, lambda *_: zeros, pipeline_mode=pl.Buffered(1))


def _ffn_tail(x, g_ref, w1_ref, w3_ref, w2_ref, o_ref):
    ms = jnp.mean(x * x, axis=-1, keepdims=True)
    h = (x * lax.rsqrt(ms + EPS) * g_ref[...]).astype(_BF16)
    acc = jnp.zeros(x.shape, _F32)
    for c in range(D_FF // FF_CHUNK):
        sl = slice(c * FF_CHUNK, (c + 1) * FF_CHUNK)
        a = _dot(h, w1_ref[:, sl])
        b = _dot(h, w3_ref[:, sl])
        gl = (a * jax.nn.sigmoid(a) * b).astype(_BF16)
        acc = acc + _dot(gl, w2_ref[sl, :])
    o_ref[...] = x + 0.5 * acc


def _ffn_kernel(x_ref, g_ref, w1_ref, w3_ref, w2_ref, o_ref):
    _ffn_tail(x_ref[...], g_ref, w1_ref, w3_ref, w2_ref, o_ref)


def _mix_ffn_kernel(x_ref, a_ref, c_ref, wo_ref, g_ref, w1_ref, w3_ref, w2_ref, o_ref):
    x = (x_ref[...] + _dot(a_ref[...], wo_ref[:ATTN_WIDTH, :])
         + _dot(c_ref[...], wo_ref[ATTN_WIDTH:, :]))
    _ffn_tail(x, g_ref, w1_ref, w3_ref, w2_ref, o_ref)


def _ffn(x2d, g, w1, w3, w2, mix=None):
    m = x2d.shape[0]
    rows = pl.BlockSpec((FFN_ROWS, D_MODEL), lambda i: (i, 0))
    w_specs = [_resident((1, D_MODEL)), _resident((D_MODEL, D_FF)),
               _resident((D_MODEL, D_FF)), _resident((D_FF, D_MODEL))]
    if mix is None:
        body, ins, specs = _ffn_kernel, (x2d,), [rows]
    else:
        attn, conv, wo = mix
        half = pl.BlockSpec((FFN_ROWS, ATTN_WIDTH), lambda i: (i, 0))
        body, ins = _mix_ffn_kernel, (x2d, attn, conv, wo)
        specs = [rows, half, half, _resident((D_MODEL, D_MODEL))]
    return pl.pallas_call(
        body,
        grid=(m // FFN_ROWS,),
        in_specs=specs + w_specs,
        out_specs=rows,
        out_shape=jax.ShapeDtypeStruct((m, D_MODEL), _F32),
        compiler_params=pltpu.CompilerParams(
            dimension_semantics=("arbitrary",), vmem_limit_bytes=VMEM_LIMIT),
        name="ffn_mix" if mix is not None else "ffn",
    )(*ins, g, w1, w3, w2)


def _head_rms(z, e_ref, gain):
    hi, lo = _split_bf16(z * z)
    ms = _dot(hi, e_ref[...]) + _dot(lo, e_ref[...])
    return z * lax.rsqrt(ms + EPS) * gain


def _in_proj_kernel(x_ref, g_ref, w_ref, e_ref, gq_ref, gk_ref, hmask_ref, slope_ref,
                    q_ref, k_ref, v_ref, u_ref, f_ref, km_ref):
    t = pl.program_id(1)
    nb = km_ref.shape[0]

    @pl.when(t == 0)
    def _():
        km_ref[...] = jnp.zeros(km_ref.shape, _F32)

    x = x_ref[0]
    ms = jnp.mean(x * x, axis=-1, keepdims=True)
    h = (x * lax.rsqrt(ms + EPS) * g_ref[...]).astype(_BF16)
    a = ATTN_WIDTH
    qn = _head_rms(_dot(h, w_ref[:, 0:a]), e_ref, gq_ref[...])
    kn = _head_rms(_dot(h, w_ref[:, a:2 * a]), e_ref, gk_ref[...])
    q_ref[0] = (qn * (HEAD_DIM ** -0.5)).astype(_BF16)
    k_ref[0] = kn.astype(_BF16)
    v_ref[0] = _dot(h, w_ref[:, 2 * a:3 * a]).astype(_BF16)
    za = _dot(h, w_ref[:, 3 * a:3 * a + CONV_WIDTH])
    zg = _dot(h, w_ref[:, 3 * a + CONV_WIDTH:])
    u_ref[0] = (za * jax.nn.sigmoid(zg)).astype(_BF16)

    n_idx = lax.broadcasted_iota(jnp.int32, (N_HEADS, nb, BLOCK), 1)
    slope_blk = slope_ref[...].reshape(N_HEADS, nb, 1)
    extra_rows = lax.broadcasted_iota(jnp.int32, (LANES - 2 * nb, BLOCK), 0)
    extra = jnp.where(extra_rows < 2, 1.0, 0.0).astype(_F32)
    for sb in range(PROJ_ROWS // BLOCK):
        i = t * (PROJ_ROWS // BLOCK) + sb
        rows = slice(sb * BLOCK, (sb + 1) * BLOCK)
        km = km_ref[...]
        kmt = jnp.concatenate([km] * N_HEADS, axis=0) * hmask_ref[...]
        km_hi, km_lo = _split_bf16(kmt)
        q_hi, q_lo = _split_bf16(qn[rows])
        gt = _dot_nt(km_hi, q_hi) + _dot_nt(km_hi, q_lo) + _dot_nt(km_lo, q_hi)
        g3 = gt.reshape(N_HEADS, nb, BLOCK)
        rank = jnp.zeros(g3.shape, jnp.int32)
        for m in range(nb):
            gm = g3[:, m:m + 1, :]
            beats = (gm > g3) | ((gm == g3) & (m < n_idx))
            rank = rank + jnp.where(beats & (m < i), 1, 0)
        sel = (n_idx < i) & (rank < TOP_K)
        alibi_blk = slope_blk * (n_idx - i).astype(_F32)
        bias = jnp.where(sel, alibi_blk, jnp.where(n_idx == i, 0.0, MASK_BIAS))
        bias2 = bias.reshape(N_HEADS * nb, BLOCK)
        for p in range(N_PAIRS):
            ft = jnp.concatenate([bias2[2 * nb * p:2 * nb * (p + 1)], extra], axis=0)
            f_ref[0, p, rows, :] = ft.T.astype(_BF16)
        km_ref[pl.ds(i, 1), :] = jnp.mean(kn[rows], axis=0, keepdims=True)


def _in_proj(x1, g, w_in, gq, gk):
    b, s, _ = x1.shape
    nb = s // BLOCK
    assert 2 * nb + 2 <= LANES
    head_of_col = np.arange(ATTN_WIDTH) // HEAD_DIM
    e = jnp.asarray((head_of_col[:, None] == head_of_col[None, :]) / HEAD_DIM, _BF16)
    hmask = jnp.asarray((np.arange(N_HEADS * nb) // nb)[:, None] == head_of_col[None, :], _F32)
    slopes = np.exp2(-ALIBI_MAX * (np.arange(N_HEADS) + 1.0) / N_HEADS)
    slope_blk = jnp.asarray(np.repeat(slopes * BLOCK, nb)[:, None], _F32)
    tile = lambda w: pl.BlockSpec((1, PROJ_ROWS, w), lambda bi, ti: (bi, ti, 0))
    act = jax.ShapeDtypeStruct((b, s, ATTN_WIDTH), _BF16)
    return pl.pallas_call(
        _in_proj_kernel,
        grid=(b, s // PROJ_ROWS),
        in_specs=[tile(D_MODEL), _resident((1, D_MODEL)), _resident((D_MODEL, IN_COLS)),
                  _resident((ATTN_WIDTH, ATTN_WIDTH)), _resident((1, ATTN_WIDTH)),
                  _resident((1, ATTN_WIDTH)), _resident((N_HEADS * nb, ATTN_WIDTH)),
                  _resident((N_HEADS * nb, 1))],
        out_specs=[tile(ATTN_WIDTH)] * 4
        + [pl.BlockSpec((1, N_PAIRS, PROJ_ROWS, LANES), lambda bi, ti: (bi, 0, ti, 0))],
        out_shape=[act] * 4 + [jax.ShapeDtypeStruct((b, N_PAIRS, s, LANES), _BF16)],
        scratch_shapes=[pltpu.VMEM((nb, ATTN_WIDTH), _F32)],
        compiler_params=pltpu.CompilerParams(
            dimension_semantics=("arbitrary", "arbitrary"), vmem_limit_bytes=VMEM_LIMIT),
        name="in_proj",
    )(x1, g, w_in, e, gq, gk, hmask, slope_blk)


CONV_RC = 64
CONV_SPAN = CONV_RC + 8 * ((CONV_K - 1) // 8)


def _conv_kernel(u_ref, w_ref, b_ref, lg_ref, lb_ref, o_ref, ubuf_ref, y_ref):
    t = pl.program_id(1)

    @pl.when(t == 0)
    def _():
        ubuf_ref[0:CONV_HALO, :] = jnp.zeros((CONV_HALO, CONV_WIDTH), _F32)

    @pl.when(t > 0)
    def _():
        ubuf_ref[0:CONV_HALO, :] = ubuf_ref[CONV_ROWS:CONV_ROWS + CONV_HALO, :]

    ubuf_ref[CONV_HALO:, :] = u_ref[0].astype(_F32)
    base = CONV_HALO - (CONV_K - 1)
    for lc in range(CONV_WIDTH // LANES):
        ln = slice(lc * LANES, (lc + 1) * LANES)
        for rc in range(CONV_ROWS // CONV_RC):
            r0 = rc * CONV_RC
            acc = jnp.broadcast_to(b_ref[:, ln], (CONV_RC, LANES))
            for off in range(8):
                sh = ubuf_ref[r0 + base + off:r0 + base + off + CONV_SPAN, ln]
                for k in range(off, CONV_K, 8):
                    acc = acc + w_ref[k:k + 1, ln] * sh[k - off:k - off + CONV_RC]
            y_ref[r0:r0 + CONV_RC, ln] = acc
    y = y_ref[...]
    mu = jnp.mean(y, axis=-1, keepdims=True)
    d = y - mu
    var = jnp.mean(d * d, axis=-1, keepdims=True)
    z = d * lax.rsqrt(var + EPS) * lg_ref[...] + lb_ref[...]
    o_ref[0] = (z * jax.nn.sigmoid(z)).astype(_BF16)


def _conv(u, w, bias, ln_g, ln_b):
    b, s, c = u.shape
    kpad = 8 * (-(-CONV_K // 8))
    tile = pl.BlockSpec((1, CONV_ROWS, c), lambda bi, ti: (bi, ti, 0))
    return pl.pallas_call(
        _conv_kernel,
        grid=(b, s // CONV_ROWS),
        in_specs=[tile, _resident((kpad, c)), _resident((1, c)), _resident((1, c)),
                  _resident((1, c))],
        out_specs=tile,
        out_shape=jax.ShapeDtypeStruct((b, s, c), _BF16),
        scratch_shapes=[pltpu.VMEM((CONV_ROWS + CONV_HALO, c), _F32),
                        pltpu.VMEM((CONV_ROWS, c), _F32)],
        compiler_params=pltpu.CompilerParams(
            dimension_semantics=("arbitrary", "arbitrary"), vmem_limit_bytes=VMEM_LIMIT),
        name="conv",
    )(u, jnp.pad(w, ((0, kpad - CONV_K), (0, 0))), bias, ln_g, ln_b)


def _attn_kernel(q_ref, f_ref, k_ref, v_ref, kf_ref, o_ref,
                 kaug_ref, vaug_ref, s_ref, acc_ref, m_ref):
    i = pl.program_id(2)
    nb = s_ref.shape[0]

    @pl.when(i == 0)
    def _():
        kaug_ref[:, :LANES] = k_ref[0]
        kaug_ref[:, LANES:] = kf_ref[0]
        vaug_ref[:, :LANES] = v_ref[0]
        vaug_ref[:, LANES:] = jnp.ones((vaug_ref.shape[0], LANES), _BF16)

    q = q_ref[0]
    f = f_ref[0, 0]
    lane = lax.broadcasted_iota(jnp.int32, (BLOCK, LANES), 1)
    row = lax.broadcasted_iota(jnp.int32, (BLOCK, BLOCK), 0)
    col = lax.broadcasted_iota(jnp.int32, (BLOCK, BLOCK), 1)
    zero = jnp.zeros((BLOCK, LANES), _BF16)
    outs = []
    for slot in range(2):
        head_lanes = (lane < HEAD_DIM) if slot == 0 else (lane >= HEAD_DIM)
        feat_lanes = ((lane >= slot * nb) & (lane < (slot + 1) * nb)) | (lane == 2 * nb + slot)
        qa = jnp.concatenate([jnp.where(head_lanes, q, zero), jnp.where(feat_lanes, f, zero)],
                             axis=1)
        sd = _dot_nt(qa, kaug_ref[pl.ds(pl.multiple_of(i * BLOCK, BLOCK), BLOCK), :])
        sd = jnp.where(col <= row, sd, NEG)
        s_ref[i] = sd
        m_ref[...] = jnp.maximum(sd[:, :LANES], sd[:, LANES:])

        def logits(j, carry):
            s = _dot_nt(qa, kaug_ref[pl.ds(pl.multiple_of(j * BLOCK, BLOCK), BLOCK), :])
            s_ref[j] = s
            m_ref[...] = jnp.maximum(m_ref[...], jnp.maximum(s[:, :LANES], s[:, LANES:]))
            return carry

        lax.fori_loop(0, i, logits, 0)
        m = jnp.max(m_ref[...], axis=1, keepdims=True)
        m_ref[...] = jnp.broadcast_to(m, (BLOCK, LANES))
        acc_ref[...] = jnp.zeros(acc_ref.shape, _F32)

        def weighted(j, carry):
            mb = m_ref[...]
            s = s_ref[j]
            p = jnp.concatenate([jnp.exp(s[:, :LANES] - mb), jnp.exp(s[:, LANES:] - mb)], axis=1)
            acc_ref[...] += _dot(p.astype(_BF16),
                                 vaug_ref[pl.ds(pl.multiple_of(j * BLOCK, BLOCK), BLOCK), :])
            return carry

        lax.fori_loop(0, i + 1, weighted, 0)
        acc = acc_ref[...]
        outs.append(acc[:, :LANES] / acc[:, LANES:])
    o_ref[0] = jnp.where(lane < HEAD_DIM, outs[0], outs[1]).astype(_BF16)


def _key_features(s):
    nb = s // BLOCK
    slopes = np.exp2(-ALIBI_MAX * (np.arange(N_HEADS) + 1.0) / N_HEADS)
    pos = np.arange(s)
    feat = np.zeros((N_PAIRS, s, LANES), np.float32)
    onehot = (pos[:, None] // BLOCK == np.arange(nb)[None, :]).astype(np.float32)
    feat[:, :, 0:nb] = onehot
    feat[:, :, nb:2 * nb] = onehot
    for p in range(N_PAIRS):
        for slot in range(2):
            feat[p, :, 2 * nb + slot] = slopes[2 * p + slot] * (pos % BLOCK)
    assert np.array_equal(feat.astype(_BF16).astype(np.float32), feat)
    return jnp.asarray(feat, _BF16)


def _attention(q, f, k, v):
    b, s, _ = q.shape
    nb = s // BLOCK
    kf = _key_features(s)
    qtile = pl.BlockSpec((1, BLOCK, LANES), lambda bi, p, i: (bi, i, p))
    seq = pl.BlockSpec((1, s, LANES), lambda bi, p, i: (bi, 0, p))
    return pl.pallas_call(
        _attn_kernel,
        grid=(b, N_PAIRS, nb),
        in_specs=[qtile,
                  pl.BlockSpec((1, 1, BLOCK, LANES), lambda bi, p, i: (bi, p, i, 0)),
                  seq, seq,
                  pl.BlockSpec((1, s, LANES), lambda bi, p, i: (p, 0, 0))],
        out_specs=qtile,
        out_shape=jax.ShapeDtypeStruct((b, s, ATTN_WIDTH), _BF16),
        scratch_shapes=[pltpu.VMEM((s, 2 * LANES), _BF16), pltpu.VMEM((s, 2 * LANES), _BF16),
                        pltpu.VMEM((nb, BLOCK, BLOCK), _F32), pltpu.VMEM((BLOCK, BLOCK), _F32),
                        pltpu.VMEM((BLOCK, LANES), _F32)],
        compiler_params=pltpu.CompilerParams(
            dimension_semantics=("arbitrary", "arbitrary", "arbitrary"),
            vmem_limit_bytes=VMEM_LIMIT),
        name="moba_attention",
    )(q, f, k, v, kf)


def kernel(x, ffn1_norm, ffn1_w1, ffn1_w3, ffn1_w2, mix_norm, w_in, q_norm, k_norm,
           conv_dw_w, conv_dw_b, conv_ln_g, conv_ln_b, w_out, ffn2_norm, ffn2_w1,
           ffn2_w3, ffn2_w2):
    b, s, d = x.shape
    assert d == D_MODEL and s % PROJ_ROWS == 0 and (b * s) % FFN_ROWS == 0
    bf = lambda w: w.astype(_BF16)
    row = lambda p: p.reshape(1, -1).astype(_F32)
    for l in range(ffn1_norm.shape[0]):
        x1 = _ffn(x.reshape(b * s, d), row(ffn1_norm[l]), bf(ffn1_w1[l]), bf(ffn1_w3[l]),
                  bf(ffn1_w2[l]))
        q, k, v, u, f = _in_proj(x1.reshape(b, s, d), row(mix_norm[l]), bf(w_in[l]),
                                 row(jnp.tile(q_norm[l], N_HEADS)),
                                 row(jnp.tile(k_norm[l], N_HEADS)))
        conv = _conv(u, conv_dw_w[l].astype(_F32), row(conv_dw_b[l]), row(conv_ln_g[l]),
                     row(conv_ln_b[l]))
        attn = _attention(q, f, k, v)
        x = _ffn(x1, row(ffn2_norm[l]), bf(ffn2_w1[l]), bf(ffn2_w3[l]), bf(ffn2_w2[l]),
                 mix=(attn.reshape(b * s, ATTN_WIDTH), conv.reshape(b * s, CONV_WIDTH),
                      bf(w_out[l]))).reshape(b, s, d)
    return x
```

```python
import functools

import numpy as np
import jax
import jax.numpy as jnp
from jax import lax
from jax.experimental import pallas as pl
from jax.experimental.pallas import tpu as pltpu

D_MODEL = 1024
ATTN_WIDTH = 512
CONV_WIDTH = 512
HEAD_DIM = 64
N_HEADS = 8
N_PAIRS = N_HEADS // 2
IN_COLS = 3 * ATTN_WIDTH + 2 * CONV_WIDTH
CONV_K = 31
BLOCK = 256
TOP_K = 3
D_FF = 2816
ALIBI_MAX = 8.0
EPS = 1e-6
MASK_BIAS = -float(2 ** 30)
NEG = -1e30

LANES = 128
FF_CHUNK = 256
FFN_ROWS = 512
PROJ_ROWS = 512
CONV_ROWS = 256
CONV_HALO = 32
VMEM_LIMIT = 56 * 1024 * 1024

_BF16 = jnp.bfloat16
_F32 = jnp.float32


def _dot(a, b):
    return jnp.dot(a, b, preferred_element_type=_F32)


def _dot_nt(a, b):
    return lax.dot_general(a, b, (((1,), (1,)), ((), ())), preferred_element_type=_F32)


def _split_bf16(x):
    hi = x.astype(_BF16)
    lo = (x - hi.astype(_F32)).astype(_BF16)
    return hi, lo


def _resident(shape):
    zeros = (0,) * len(shape)
    return pl.BlockSpec(shape, lambda *_: zeros, pipeline_mode=pl.Buffered(1))


def _ffn_tail(x, g_ref, w1_ref, w3_ref, w2_ref, o_ref):
    ms = jnp.mean(x * x, axis=-1, keepdims=True)
    h = (x * lax.rsqrt(ms + EPS) * g_ref[...]).astype(_BF16)
    acc = jnp.zeros(x.shape, _F32)
    for c in range(D_FF // FF_CHUNK):
        sl = slice(c * FF_CHUNK, (c + 1) * FF_CHUNK)
        a = _dot(h, w1_ref[:, sl])
        b = _dot(h, w3_ref[:, sl])
        gl = (a * jax.nn.sigmoid(a) * b).astype(_BF16)
        acc = acc + _dot(gl, w2_ref[sl, :])
    o_ref[...] = x + 0.5 * acc


def _ffn_kernel(x_ref, g_ref, w1_ref, w3_ref, w2_ref, o_ref):
    _ffn_tail(x_ref[...], g_ref, w1_ref, w3_ref, w2_ref, o_ref)


def _mix_ffn_kernel(x_ref, a_ref, c_ref, wo_ref, g_ref, w1_ref, w3_ref, w2_ref, o_ref):
    x = (x_ref[...] + _dot(a_ref[...], wo_ref[:ATTN_WIDTH, :])
         + _dot(c_ref[...], wo_ref[ATTN_WIDTH:, :]))
    _ffn_tail(x, g_ref, w1_ref, w3_ref, w2_ref, o_ref)


def _ffn(x2d, g, w1, w3, w2, mix=None):
    m = x2d.shape[0]
    rows = pl.BlockSpec((FFN_ROWS, D_MODEL), lambda i: (i, 0))
    w_specs = [_resident((1, D_MODEL)), _resident((D_MODEL, D_FF)),
               _resident((D_MODEL, D_FF)), _resident((D_FF, D_MODEL))]
    if mix is None:
        body, ins, specs = _ffn_kernel, (x2d,), [rows]
    else:
        attn, conv, wo = mix
        half = pl.BlockSpec((FFN_ROWS, ATTN_WIDTH), lambda i: (i, 0))
        body, ins = _mix_ffn_kernel, (x2d, attn, conv, wo)
        specs = [rows, half, half, _resident((D_MODEL, D_MODEL))]
    return pl.pallas_call(
        body,
        grid=(m // FFN_ROWS,),
        in_specs=specs + w_specs,
        out_specs=rows,
        out_shape=jax.ShapeDtypeStruct((m, D_MODEL), _F32),
        compiler_params=pltpu.CompilerParams(
            dimension_semantics=("arbitrary",), vmem_limit_bytes=VMEM_LIMIT),
        name="ffn_mix" if mix is not None else "ffn",
    )(*ins, g, w1, w3, w2)


def _head_rms(z, e_ref, gain):
    hi, lo = _split_bf16(z * z)
    ms = _dot(hi, e_ref[...]) + _dot(lo, e_ref[...])
    return z * lax.rsqrt(ms + EPS) * gain


def _in_proj_kernel(x_ref, g_ref, w_ref, e_ref, gq_ref, gk_ref, hmask_ref, slope_ref,
                    q_ref, k_ref, v_ref, u_ref, f_ref, km_ref):
    t = pl.program_id(1)
    nb = km_ref.shape[0]

    @pl.when(t == 0)
    def _():
        km_ref[...] = jnp.zeros(km_ref.shape, _F32)

    x = x_ref[0]
    ms = jnp.mean(x * x, axis=-1, keepdims=True)
    h = (x * lax.rsqrt(ms + EPS) * g_ref[...]).astype(_BF16)
    a = ATTN_WIDTH
    qn = _head_rms(_dot(h, w_ref[:, 0:a]), e_ref, gq_ref[...])
    kn = _head_rms(_dot(h, w_ref[:, a:2 * a]), e_ref, gk_ref[...])
    q_ref[0] = (qn * (HEAD_DIM ** -0.5)).astype(_BF16)
    k_ref[0] = kn.astype(_BF16)
    v_ref[0] = _dot(h, w_ref[:, 2 * a:3 * a]).astype(_BF16)
    za = _dot(h, w_ref[:, 3 * a:3 * a + CONV_WIDTH])
    zg = _dot(h, w_ref[:, 3 * a + CONV_WIDTH:])
    u_ref[0] = (za * jax.nn.sigmoid(zg)).astype(_BF16)

    n_idx = lax.broadcasted_iota(jnp.int32, (N_HEADS, nb, BLOCK), 1)
    slope_blk = slope_ref[...].reshape(N_HEADS, nb, 1)
    extra_rows = lax.broadcasted_iota(jnp.int32, (LANES - 2 * nb, BLOCK), 0)
    extra = jnp.where(extra_rows < 2, 1.0, 0.0).astype(_F32)
    for sb in range(PROJ_ROWS // BLOCK):
        i = t * (PROJ_ROWS // BLOCK) + sb
        rows = slice(sb * BLOCK, (sb + 1) * BLOCK)
        km = km_ref[...]
        kmt = jnp.concatenate([km] * N_HEADS, axis=0) * hmask_ref[...]
        km_hi, km_lo = _split_bf16(kmt)
        q_hi, q_lo = _split_bf16(qn[rows])
        gt = _dot_nt(km_hi, q_hi) + _dot_nt(km_hi, q_lo) + _dot_nt(km_lo, q_hi)
        g3 = gt.reshape(N_HEADS, nb, BLOCK)
        rank = jnp.zeros(g3.shape, jnp.int32)
        for m in range(nb):
            gm = g3[:, m:m + 1, :]
            beats = (gm > g3) | ((gm == g3) & (m < n_idx))
            rank = rank + jnp.where(beats & (m < i), 1, 0)
        sel = (n_idx < i) & (rank < TOP_K)
        alibi_blk = slope_blk * (n_idx - i).astype(_F32)
        bias = jnp.where(sel, alibi_blk, jnp.where(n_idx == i, 0.0, MASK_BIAS))
        bias2 = bias.reshape(N_HEADS * nb, BLOCK)
        for p in range(N_PAIRS):
            ft = jnp.concatenate([bias2[2 * nb * p:2 * nb * (p + 1)], extra], axis=0)
            f_ref[0, p, rows, :] = ft.T.astype(_BF16)
        km_ref[pl.ds(i, 1), :] = jnp.mean(kn[rows], axis=0, keepdims=True)


def _in_proj(x1, g, w_in, gq, gk):
    b, s, _ = x1.shape
    nb = s // BLOCK
    assert 2 * nb + 2 <= LANES
    head_of_col = np.arange(ATTN_WIDTH) // HEAD_DIM
    e = jnp.asarray((head_of_col[:, None] == head_of_col[None, :]) / HEAD_DIM, _BF16)
    hmask = jnp.asarray((np.arange(N_HEADS * nb) // nb)[:, None] == head_of_col[None, :], _F32)
    slopes = np.exp2(-ALIBI_MAX * (np.arange(N_HEADS) + 1.0) / N_HEADS)
    slope_blk = jnp.asarray(np.repeat(slopes * BLOCK, nb)[:, None], _F32)
    tile = lambda w: pl.BlockSpec((1, PROJ_ROWS, w), lambda bi, ti: (bi, ti, 0))
    act = jax.ShapeDtypeStruct((b, s, ATTN_WIDTH), _BF16)
    return pl.pallas_call(
        _in_proj_kernel,
        grid=(b, s // PROJ_ROWS),
        in_specs=[tile(D_MODEL), _resident((1, D_MODEL)), _resident((D_MODEL, IN_COLS)),
                  _resident((ATTN_WIDTH, ATTN_WIDTH)), _resident((1, ATTN_WIDTH)),
                  _resident((1, ATTN_WIDTH)), _resident((N_HEADS * nb, ATTN_WIDTH)),
                  _resident((N_HEADS * nb, 1))],
        out_specs=[tile(ATTN_WIDTH)] * 4
        + [pl.BlockSpec((1, N_PAIRS, PROJ_ROWS, LANES), lambda bi, ti: (bi, 0, ti, 0))],
        out_shape=[act] * 4 + [jax.ShapeDtypeStruct((b, N_PAIRS, s, LANES), _BF16)],
        scratch_shapes=[pltpu.VMEM((nb, ATTN_WIDTH), _F32)],
        compiler_params=pltpu.CompilerParams(
            dimension_semantics=("arbitrary", "arbitrary"), vmem_limit_bytes=VMEM_LIMIT),
        name="in_proj",
    )(x1, g, w_in, e, gq, gk, hmask, slope_blk)


CONV_RC = 64
CONV_SPAN = CONV_RC + 8 * ((CONV_K - 1) // 8)


def _conv_kernel(u_ref, w_ref, b_ref, lg_ref, lb_ref, o_ref, ubuf_ref, y_ref):
    t = pl.program_id(1)

    @pl.when(t == 0)
    def _():
        ubuf_ref[0:CONV_HALO, :] = jnp.zeros((CONV_HALO, CONV_WIDTH), _F32)

    @pl.when(t > 0)
    def _():
        ubuf_ref[0:CONV_HALO, :] = ubuf_ref[CONV_ROWS:CONV_ROWS + CONV_HALO, :]

    ubuf_ref[CONV_HALO:, :] = u_ref[0].astype(_F32)
    base = CONV_HALO - (CONV_K - 1)
    for lc in range(CONV_WIDTH // LANES):
        ln = slice(lc * LANES, (lc + 1) * LANES)
        for rc in range(CONV_ROWS // CONV_RC):
            r0 = rc * CONV_RC
            acc = jnp.broadcast_to(b_ref[:, ln], (CONV_RC, LANES))
            for off in range(8):
                sh = ubuf_ref[r0 + base + off:r0 + base + off + CONV_SPAN, ln]
                for k in range(off, CONV_K, 8):
                    acc = acc + w_ref[k:k + 1, ln] * sh[k - off:k - off + CONV_RC]
            y_ref[r0:r0 + CONV_RC, ln] = acc
    y = y_ref[...]
    mu = jnp.mean(y, axis=-1, keepdims=True)
    d = y - mu
    var = jnp.mean(d * d, axis=-1, keepdims=True)
    z = d * lax.rsqrt(var + EPS) * lg_ref[...] + lb_ref[...]
    o_ref[0] = (z * jax.nn.sigmoid(z)).astype(_BF16)


def _conv(u, w, bias, ln_g, ln_b):
    b, s, c = u.shape
    kpad = 8 * (-(-CONV_K // 8))
    tile = pl.BlockSpec((1, CONV_ROWS, c), lambda bi, ti: (bi, ti, 0))
    return pl.pallas_call(
        _conv_kernel,
        grid=(b, s // CONV_ROWS),
        in_specs=[tile, _resident((kpad, c)), _resident((1, c)), _resident((1, c)),
                  _resident((1, c))],
        out_specs=tile,
        out_shape=jax.ShapeDtypeStruct((b, s, c), _BF16),
        scratch_shapes=[pltpu.VMEM((CONV_ROWS + CONV_HALO, c), _F32),
                        pltpu.VMEM((CONV_ROWS, c), _F32)],
        compiler_params=pltpu.CompilerParams(
            dimension_semantics=("arbitrary", "arbitrary"), vmem_limit_bytes=VMEM_LIMIT),
        name="conv",
    )(u, jnp.pad(w, ((0, kpad - CONV_K), (0, 0))), bias, ln_g, ln_b)


def _attn_kernel(q_ref, f_ref, k_ref, v_ref, kf_ref, o_ref, kaug_ref, vaug_ref, s_ref):
    slot = pl.program_id(2)
    nb = s_ref.shape[2] // BLOCK

    @pl.when(slot == 0)
    def _():
        kaug_ref[:, :LANES] = k_ref[0]
        kaug_ref[:, LANES:] = kf_ref[0]
        vaug_ref[:, :LANES] = v_ref[0]
        vaug_ref[:, LANES:] = jnp.ones((vaug_ref.shape[0], LANES), _BF16)

    lane = lax.broadcasted_iota(jnp.int32, (BLOCK, LANES), 1)
    causal = (lax.broadcasted_iota(jnp.int32, (BLOCK, BLOCK), 1)
              <= lax.broadcasted_iota(jnp.int32, (BLOCK, BLOCK), 0))
    head_lanes = (lane >= slot * HEAD_DIM) & (lane < (slot + 1) * HEAD_DIM)
    feat_lanes = ((lane >= slot * nb) & (lane < (slot + 1) * nb)) | (lane == 2 * nb + slot)
    zero = jnp.zeros((BLOCK, LANES), _BF16)
    for i in range(nb):
        rows = slice(i * BLOCK, (i + 1) * BLOCK)
        n = (i + 1) * BLOCK
        buf = i % 2
        qa = jnp.concatenate([jnp.where(head_lanes, q_ref[0, rows, :], zero),
                              jnp.where(feat_lanes, f_ref[0, 0, rows, :], zero)], axis=1)
        s = _dot_nt(qa, kaug_ref[0:n, :])
        if i:
            s_ref[buf, :, 0:i * BLOCK] = s[:, 0:i * BLOCK]
        s_ref[buf, :, i * BLOCK:n] = jnp.where(causal, s[:, i * BLOCK:n], NEG)
        m = jnp.max(s_ref[buf, :, 0:n], axis=1, keepdims=True)
        p = jnp.exp(s_ref[buf, :, 0:n] - m).astype(_BF16)
        acc = _dot(p, vaug_ref[0:n, :])
        out = (acc[:, :LANES] / acc[:, LANES:]).astype(_BF16)

        @pl.when(slot == 0)
        def _():
            o_ref[0, rows, :] = out

        @pl.when(slot == 1)
        def _():
            o_ref[0, rows, :] = jnp.where(lane < HEAD_DIM, o_ref[0, rows, :], out)


def _key_features(s):
    nb = s // BLOCK
    slopes = np.exp2(-ALIBI_MAX * (np.arange(N_HEADS) + 1.0) / N_HEADS)
    pos = np.arange(s)
    feat = np.zeros((N_PAIRS, s, LANES), np.float32)
    onehot = (pos[:, None] // BLOCK == np.arange(nb)[None, :]).astype(np.float32)
    feat[:, :, 0:nb] = onehot
    feat[:, :, nb:2 * nb] = onehot
    for p in range(N_PAIRS):
        for slot in range(2):
            feat[p, :, 2 * nb + slot] = slopes[2 * p + slot] * (pos % BLOCK)
    assert np.array_equal(feat.astype(_BF16).astype(np.float32), feat)
    return jnp.asarray(feat, _BF16)


def _attention(q, f, k, v):
    b, s, _ = q.shape
    kf = _key_features(s)
    seq = pl.BlockSpec((1, s, LANES), lambda bi, p, sl: (bi, 0, p))
    return pl.pallas_call(
        _attn_kernel,
        grid=(b, N_PAIRS, 2),
        in_specs=[seq,
                  pl.BlockSpec((1, 1, s, LANES), lambda bi, p, sl: (bi, p, 0, 0)),
                  seq, seq,
                  pl.BlockSpec((1, s, LANES), lambda bi, p, sl: (p, 0, 0))],
        out_specs=seq,
        out_shape=jax.ShapeDtypeStruct((b, s, ATTN_WIDTH), _BF16),
        scratch_shapes=[pltpu.VMEM((s, 2 * LANES), _BF16), pltpu.VMEM((s, 2 * LANES), _BF16),
                        pltpu.VMEM((2, BLOCK, s), _F32)],
        compiler_params=pltpu.CompilerParams(
            dimension_semantics=("arbitrary", "arbitrary", "arbitrary"),
            vmem_limit_bytes=VMEM_LIMIT),
        name="moba_attention",
    )(q, f, k, v, kf)


def kernel(x, ffn1_norm, ffn1_w1, ffn1_w3, ffn1_w2, mix_norm, w_in, q_norm, k_norm,
           conv_dw_w, conv_dw_b, conv_ln_g, conv_ln_b, w_out, ffn2_norm, ffn2_w1,
           ffn2_w3, ffn2_w2):
    b, s, d = x.shape
    assert d == D_MODEL and s % PROJ_ROWS == 0 and (b * s) % FFN_ROWS == 0
    bf = lambda w: w.astype(_BF16)
    row = lambda p: p.reshape(1, -1).astype(_F32)
    for l in range(ffn1_norm.shape[0]):
        x1 = _ffn(x.reshape(b * s, d), row(ffn1_norm[l]), bf(ffn1_w1[l]), bf(ffn1_w3[l]),
                  bf(ffn1_w2[l]))
        q, k, v, u, f = _in_proj(x1.reshape(b, s, d), row(mix_norm[l]), bf(w_in[l]),
                                 row(jnp.tile(q_norm[l], N_HEADS)),
                                 row(jnp.tile(k_norm[l], N_HEADS)))
        conv = _conv(u, conv_dw_w[l].astype(_F32), row(conv_dw_b[l]), row(conv_ln_g[l]),
                     row(conv_ln_b[l]))
        attn = _attention(q, f, k, v)
        x = _ffn(x1, row(ffn2_norm[l]), bf(ffn2_w1[l]), bf(ffn2_w3[l]), bf(ffn2_w2[l]),
                 mix=(attn.reshape(b * s, ATTN_WIDTH), conv.reshape(b * s, CONV_WIDTH),
                      bf(w_out[l]))).reshape(b, s, d)
    return x
```

```python
import functools

import numpy as np
import jax
import jax.numpy as jnp
from jax import lax
from jax.experimental import pallas as pl
from jax.experimental.pallas import tpu as pltpu

D_MODEL = 1024
ATTN_WIDTH = 512
CONV_WIDTH = 512
HEAD_DIM = 64
N_HEADS = 8
N_PAIRS = N_HEADS // 2
IN_COLS = 3 * ATTN_WIDTH + 2 * CONV_WIDTH
CONV_K = 31
BLOCK = 256
TOP_K = 3
D_FF = 2816
ALIBI_MAX = 8.0
EPS = 1e-6
MASK_BIAS = -float(2 ** 30)
NEG = -1e30
MIN_ROW_SUM = 2.0 ** -58

LANES = 128
FF_CHUNK = 256
FFN_ROWS = 512
PROJ_ROWS = 512
CONV_ROWS = 256
CONV_HALO = 32
VMEM_LIMIT = 56 * 1024 * 1024

_BF16 = jnp.bfloat16
_F32 = jnp.float32


def _dot(a, b):
    return jnp.dot(a, b, preferred_element_type=_F32)


def _dot_nt(a, b):
    return lax.dot_general(a, b, (((1,), (1,)), ((), ())), preferred_element_type=_F32)


def _split_bf16(x):
    hi = x.astype(_BF16)
    lo = (x - hi.astype(_F32)).astype(_BF16)
    return hi, lo


def _alibi_slopes():
    return np.exp2(-ALIBI_MAX * (np.arange(N_HEADS) + 1.0) / N_HEADS)


def _resident(shape):
    zeros = (0,) * len(shape)
    return pl.BlockSpec(shape, lambda *_: zeros, pipeline_mode=pl.Buffered(1))


def _ffn_tail(x, g_ref, w1_ref, w3_ref, w2_ref, o_ref):
    ms = jnp.mean(x * x, axis=-1, keepdims=True)
    h = (x * lax.rsqrt(ms + EPS) * g_ref[...]).astype(_BF16)
    acc = jnp.zeros(x.shape, _F32)
    for c in range(D_FF // FF_CHUNK):
        sl = slice(c * FF_CHUNK, (c + 1) * FF_CHUNK)
        a = _dot(h, w1_ref[:, sl])
        b = _dot(h, w3_ref[:, sl])
        gl = (a * jax.nn.sigmoid(a) * b).astype(_BF16)
        acc = acc + _dot(gl, w2_ref[sl, :])
    o_ref[...] = x + 0.5 * acc


def _ffn_kernel(x_ref, g_ref, w1_ref, w3_ref, w2_ref, o_ref):
    _ffn_tail(x_ref[...], g_ref, w1_ref, w3_ref, w2_ref, o_ref)


def _mix_ffn_kernel(x_ref, a_ref, c_ref, wo_ref, g_ref, w1_ref, w3_ref, w2_ref, o_ref):
    x = (x_ref[...] + _dot(a_ref[...], wo_ref[:ATTN_WIDTH, :])
         + _dot(c_ref[...], wo_ref[ATTN_WIDTH:, :]))
    _ffn_tail(x, g_ref, w1_ref, w3_ref, w2_ref, o_ref)


def _ffn(x2d, g, w1, w3, w2, mix=None):
    m = x2d.shape[0]
    rows = pl.BlockSpec((FFN_ROWS, D_MODEL), lambda i: (i, 0))
    w_specs = [_resident((1, D_MODEL)), _resident((D_MODEL, D_FF)),
               _resident((D_MODEL, D_FF)), _resident((D_FF, D_MODEL))]
    if mix is None:
        body, ins, specs = _ffn_kernel, (x2d,), [rows]
    else:
        attn, conv, wo = mix
        half = pl.BlockSpec((FFN_ROWS, ATTN_WIDTH), lambda i: (i, 0))
        body, ins = _mix_ffn_kernel, (x2d, attn, conv, wo)
        specs = [rows, half, half, _resident((D_MODEL, D_MODEL))]
    return pl.pallas_call(
        body,
        grid=(m // FFN_ROWS,),
        in_specs=specs + w_specs,
        out_specs=rows,
        out_shape=jax.ShapeDtypeStruct((m, D_MODEL), _F32),
        compiler_params=pltpu.CompilerParams(
            dimension_semantics=("arbitrary",), vmem_limit_bytes=VMEM_LIMIT),
        name="ffn_mix" if mix is not None else "ffn",
    )(*ins, g, w1, w3, w2)


def _head_rms(z, e_ref, gain):
    hi, lo = _split_bf16(z * z)
    ms = _dot(hi, e_ref[...]) + _dot(lo, e_ref[...])
    return z * lax.rsqrt(ms + EPS) * gain


def _in_proj_kernel(x_ref, g_ref, w_ref, e_ref, gq_ref, gk_ref, hmask_ref, slope_ref,
                    q_ref, k_ref, v_ref, u_ref, f_ref, km_ref):
    t = pl.program_id(1)
    nb = km_ref.shape[0]

    @pl.when(t == 0)
    def _():
        km_ref[...] = jnp.zeros(km_ref.shape, _F32)

    x = x_ref[0]
    ms = jnp.mean(x * x, axis=-1, keepdims=True)
    h = (x * lax.rsqrt(ms + EPS) * g_ref[...]).astype(_BF16)
    a = ATTN_WIDTH
    qn = _head_rms(_dot(h, w_ref[:, 0:a]), e_ref, gq_ref[...])
    kn = _head_rms(_dot(h, w_ref[:, a:2 * a]), e_ref, gk_ref[...])
    q_ref[0] = (qn * (HEAD_DIM ** -0.5)).astype(_BF16)
    k_ref[0] = kn.astype(_BF16)
    v_ref[0] = _dot(h, w_ref[:, 2 * a:3 * a]).astype(_BF16)
    za = _dot(h, w_ref[:, 3 * a:3 * a + CONV_WIDTH])
    zg = _dot(h, w_ref[:, 3 * a + CONV_WIDTH:])
    u_ref[0] = (za * jax.nn.sigmoid(zg)).astype(_BF16)

    n_idx = lax.broadcasted_iota(jnp.int32, (N_HEADS, nb, BLOCK), 1)
    slope_blk = slope_ref[...].reshape(N_HEADS, nb, 1)
    extra_rows = lax.broadcasted_iota(jnp.int32, (LANES - 2 * nb, BLOCK), 0)
    q_off = lax.broadcasted_iota(jnp.int32, (LANES - 2 * nb, BLOCK), 1).astype(_F32)
    slopes = _alibi_slopes()
    extras = [jnp.where(extra_rows < 2, 1.0,
                        jnp.where(extra_rows == 2, -float(slopes[2 * p]) * q_off,
                                  jnp.where(extra_rows == 3, -float(slopes[2 * p + 1]) * q_off, 0.0)))
              for p in range(N_PAIRS)]
    for sb in range(PROJ_ROWS // BLOCK):
        i = t * (PROJ_ROWS // BLOCK) + sb
        rows = slice(sb * BLOCK, (sb + 1) * BLOCK)
        km = km_ref[...]
        kmt = jnp.concatenate([km] * N_HEADS, axis=0) * hmask_ref[...]
        km_hi, km_lo = _split_bf16(kmt)
        q_hi, q_lo = _split_bf16(qn[rows])
        gt = _dot_nt(km_hi, q_hi) + _dot_nt(km_hi, q_lo) + _dot_nt(km_lo, q_hi)
        g3 = gt.reshape(N_HEADS, nb, BLOCK)
        rank = jnp.zeros(g3.shape, jnp.int32)
        for m in range(nb):
            gm = g3[:, m:m + 1, :]
            beats = (gm > g3) | ((gm == g3) & (m < n_idx))
            rank = rank + jnp.where(beats & (m < i), 1, 0)
        sel = (n_idx < i) & (rank < TOP_K)
        alibi_blk = slope_blk * (n_idx - i).astype(_F32)
        bias = jnp.where(sel, alibi_blk, jnp.where(n_idx == i, 0.0, MASK_BIAS))
        bias2 = bias.reshape(N_HEADS * nb, BLOCK)
        for p in range(N_PAIRS):
            ft = jnp.concatenate([bias2[2 * nb * p:2 * nb * (p + 1)], extras[p]], axis=0)
            f_ref[0, p, rows, :] = ft.T.astype(_BF16)
        km_ref[pl.ds(i, 1), :] = jnp.mean(kn[rows], axis=0, keepdims=True)


def _in_proj(x1, g, w_in, gq, gk):
    b, s, _ = x1.shape
    nb = s // BLOCK
    assert 2 * nb + 4 <= LANES
    head_of_col = np.arange(ATTN_WIDTH) // HEAD_DIM
    e = jnp.asarray((head_of_col[:, None] == head_of_col[None, :]) / HEAD_DIM, _BF16)
    hmask = jnp.asarray((np.arange(N_HEADS * nb) // nb)[:, None] == head_of_col[None, :], _F32)
    slope_blk = jnp.asarray(np.repeat(_alibi_slopes() * BLOCK, nb)[:, None], _F32)
    tile = lambda w: pl.BlockSpec((1, PROJ_ROWS, w), lambda bi, ti: (bi, ti, 0))
    act = jax.ShapeDtypeStruct((b, s, ATTN_WIDTH), _BF16)
    return pl.pallas_call(
        _in_proj_kernel,
        grid=(b, s // PROJ_ROWS),
        in_specs=[tile(D_MODEL), _resident((1, D_MODEL)), _resident((D_MODEL, IN_COLS)),
                  _resident((ATTN_WIDTH, ATTN_WIDTH)), _resident((1, ATTN_WIDTH)),
                  _resident((1, ATTN_WIDTH)), _resident((N_HEADS * nb, ATTN_WIDTH)),
                  _resident((N_HEADS * nb, 1))],
        out_specs=[tile(ATTN_WIDTH)] * 4
        + [pl.BlockSpec((1, N_PAIRS, PROJ_ROWS, LANES), lambda bi, ti: (bi, 0, ti, 0))],
        out_shape=[act] * 4 + [jax.ShapeDtypeStruct((b, N_PAIRS, s, LANES), _BF16)],
        scratch_shapes=[pltpu.VMEM((nb, ATTN_WIDTH), _F32)],
        compiler_params=pltpu.CompilerParams(
            dimension_semantics=("arbitrary", "arbitrary"), vmem_limit_bytes=VMEM_LIMIT),
        name="in_proj",
    )(x1, g, w_in, e, gq, gk, hmask, slope_blk)


CONV_RC = 64
CONV_SPAN = CONV_RC + 8 * ((CONV_K - 1) // 8)


def _conv_kernel(u_ref, w_ref, b_ref, lg_ref, lb_ref, o_ref, ubuf_ref, y_ref):
    t = pl.program_id(1)

    @pl.when(t == 0)
    def _():
        ubuf_ref[0:CONV_HALO, :] = jnp.zeros((CONV_HALO, CONV_WIDTH), _F32)

    @pl.when(t > 0)
    def _():
        ubuf_ref[0:CONV_HALO, :] = ubuf_ref[CONV_ROWS:CONV_ROWS + CONV_HALO, :]

    ubuf_ref[CONV_HALO:, :] = u_ref[0].astype(_F32)
    base = CONV_HALO - (CONV_K - 1)
    for lc in range(CONV_WIDTH // LANES):
        ln = slice(lc * LANES, (lc + 1) * LANES)
        for rc in range(CONV_ROWS // CONV_RC):
            r0 = rc * CONV_RC
            acc = jnp.broadcast_to(b_ref[:, ln], (CONV_RC, LANES))
            for off in range(8):
                sh = ubuf_ref[r0 + base + off:r0 + base + off + CONV_SPAN, ln]
                for k in range(off, CONV_K, 8):
                    acc = acc + w_ref[k:k + 1, ln] * sh[k - off:k - off + CONV_RC]
            y_ref[r0:r0 + CONV_RC, ln] = acc
    y = y_ref[...]
    mu = jnp.mean(y, axis=-1, keepdims=True)
    d = y - mu
    var = jnp.mean(d * d, axis=-1, keepdims=True)
    z = d * lax.rsqrt(var + EPS) * lg_ref[...] + lb_ref[...]
    o_ref[0] = (z * jax.nn.sigmoid(z)).astype(_BF16)


def _conv(u, w, bias, ln_g, ln_b):
    b, s, c = u.shape
    kpad = 8 * (-(-CONV_K // 8))
    tile = pl.BlockSpec((1, CONV_ROWS, c), lambda bi, ti: (bi, ti, 0))
    return pl.pallas_call(
        _conv_kernel,
        grid=(b, s // CONV_ROWS),
        in_specs=[tile, _resident((kpad, c)), _resident((1, c)), _resident((1, c)),
                  _resident((1, c))],
        out_specs=tile,
        out_shape=jax.ShapeDtypeStruct((b, s, c), _BF16),
        scratch_shapes=[pltpu.VMEM((CONV_ROWS + CONV_HALO, c), _F32),
                        pltpu.VMEM((CONV_ROWS, c), _F32)],
        compiler_params=pltpu.CompilerParams(
            dimension_semantics=("arbitrary", "arbitrary"), vmem_limit_bytes=VMEM_LIMIT),
        name="conv",
    )(u, jnp.pad(w, ((0, kpad - CONV_K), (0, 0))), bias, ln_g, ln_b)


def _stacked_queries(q, f, nb):
    lane = lax.broadcasted_iota(jnp.int32, (BLOCK, LANES), 1)
    zero = jnp.zeros((BLOCK, LANES), _BF16)
    halves = []
    for sl in range(2):
        head_lanes = (lane >= sl * HEAD_DIM) & (lane < (sl + 1) * HEAD_DIM)
        feat_lanes = (((lane >= sl * nb) & (lane < (sl + 1) * nb)) | (lane == 2 * nb + sl)
                      | (lane == 2 * nb + 2 + sl))
        halves.append(jnp.concatenate([jnp.where(head_lanes, q, zero),
                                       jnp.where(feat_lanes, f, zero)], axis=1))
    return jnp.concatenate(halves, axis=0)


def _causal_mask():
    return (lax.broadcasted_iota(jnp.int32, (2 * BLOCK, BLOCK), 1)
            <= lax.broadcasted_iota(jnp.int32, (2 * BLOCK, BLOCK), 0) % BLOCK)


def _merge_heads(acc):
    lane = lax.broadcasted_iota(jnp.int32, (BLOCK, LANES), 1)
    out = acc[:, :LANES] / acc[:, LANES:]
    return jnp.where(lane < HEAD_DIM, out[:BLOCK], out[BLOCK:]).astype(_BF16)


def _attn_row_block(i, nb, shift, q_ref, f_ref, o_ref, kaug_ref, vaug_ref):
    rows = slice(i * BLOCK, (i + 1) * BLOCK)
    qa = _stacked_queries(q_ref[0, rows, :], f_ref[0, 0, rows, :], nb)
    probs = []
    for j in range(i + 1):
        s = _dot_nt(qa, kaug_ref[j * BLOCK:(j + 1) * BLOCK, :])
        if j == i:
            s = jnp.where(_causal_mask(), s, NEG)
        probs.append(jnp.exp(s - shift).astype(_BF16))
    acc = _dot(jnp.concatenate(probs, axis=1), vaug_ref[0:(i + 1) * BLOCK, :])
    o_ref[0, rows, :] = _merge_heads(acc)
    return acc[:, LANES:]


def _attn_row_block_running_max(i, nb, q_ref, f_ref, o_ref, kaug_ref, vaug_ref, acc_ref, m_ref):
    rows = pl.ds(pl.multiple_of(i * BLOCK, BLOCK), BLOCK)
    qa = _stacked_queries(q_ref[0, rows, :], f_ref[0, 0, rows, :], nb)
    m_ref[...] = jnp.full(m_ref.shape, NEG, _F32)
    acc_ref[...] = jnp.zeros(acc_ref.shape, _F32)

    def key_block(j, carry):
        keys = pl.ds(pl.multiple_of(j * BLOCK, BLOCK), BLOCK)
        s = _dot_nt(qa, kaug_ref[keys, :])
        s = jnp.where(_causal_mask() | (j < i), s, NEG)
        m_old = m_ref[...]
        m_new = jnp.maximum(m_old, jnp.max(s, axis=1, keepdims=True))
        m_ref[...] = m_new
        p = jnp.exp(s - jnp.concatenate([m_new, m_new], axis=1)).astype(_BF16)
        rescale = jnp.exp(m_old - m_new)
        acc_ref[...] = (acc_ref[...] * jnp.concatenate([rescale, rescale], axis=1)
                        + _dot(p, vaug_ref[keys, :]))
        return carry

    lax.fori_loop(0, i + 1, key_block, 0)
    o_ref[0, rows, :] = _merge_heads(acc_ref[...])


def _attn_kernel(shift_ref, q_ref, f_ref, k_ref, v_ref, kf_ref, o_ref,
                 kaug_ref, vaug_ref, acc_ref, m_ref):
    nb = kaug_ref.shape[0] // BLOCK
    kaug_ref[:, :LANES] = k_ref[0]
    kaug_ref[:, LANES:] = kf_ref[0]
    vaug_ref[:, :LANES] = v_ref[0]
    vaug_ref[:, LANES:] = jnp.ones((vaug_ref.shape[0], LANES), _BF16)
    shift = shift_ref[0, 0]
    min_sum = None
    for i in range(nb):
        sums = _attn_row_block(i, nb, shift, q_ref, f_ref, o_ref, kaug_ref, vaug_ref)
        min_sum = sums if min_sum is None else jnp.minimum(min_sum, sums)
    healthy = jnp.min(min_sum) >= MIN_ROW_SUM

    @pl.when(jnp.logical_not(healthy))
    def _():
        def row_block(i, carry):
            _attn_row_block_running_max(i, nb, q_ref, f_ref, o_ref, kaug_ref, vaug_ref,
                                        acc_ref, m_ref)
            return carry

        lax.fori_loop(0, nb, row_block, 0)


def _key_features(s):
    nb = s // BLOCK
    slopes = _alibi_slopes()
    pos = np.arange(s)
    feat = np.zeros((N_PAIRS, s, LANES), np.float32)
    onehot = (pos[:, None] // BLOCK == np.arange(nb)[None, :]).astype(np.float32)
    feat[:, :, 0:nb] = onehot
    feat[:, :, nb:2 * nb] = onehot
    for p in range(N_PAIRS):
        for slot in range(2):
            feat[p, :, 2 * nb + slot] = slopes[2 * p + slot] * (pos % BLOCK)
            feat[p, :, 2 * nb + 2 + slot] = 1.0
    assert np.array_equal(feat.astype(_BF16).astype(np.float32), feat)
    return jnp.asarray(feat, _BF16)


def _attention(q, f, k, v, shift):
    b, s, _ = q.shape
    kf = _key_features(s)
    seq = pl.BlockSpec((1, s, LANES), lambda bi, p: (bi, 0, p))
    return pl.pallas_call(
        _attn_kernel,
        grid=(b, N_PAIRS),
        in_specs=[pl.BlockSpec(memory_space=pltpu.SMEM), seq,
                  pl.BlockSpec((1, 1, s, LANES), lambda bi, p: (bi, p, 0, 0)),
                  seq, seq,
                  pl.BlockSpec((1, s, LANES), lambda bi, p: (p, 0, 0))],
        out_specs=seq,
        out_shape=jax.ShapeDtypeStruct((b, s, ATTN_WIDTH), _BF16),
        scratch_shapes=[pltpu.VMEM((s, 2 * LANES), _BF16), pltpu.VMEM((s, 2 * LANES), _BF16),
                        pltpu.VMEM((2 * BLOCK, 2 * LANES), _F32),
                        pltpu.VMEM((2 * BLOCK, LANES), _F32)],
        compiler_params=pltpu.CompilerParams(
            dimension_semantics=("arbitrary", "arbitrary"), vmem_limit_bytes=VMEM_LIMIT),
        name="moba_attention",
    )(shift, q, f, k, v, kf)


def kernel(x, ffn1_norm, ffn1_w1, ffn1_w3, ffn1_w2, mix_norm, w_in, q_norm, k_norm,
           conv_dw_w, conv_dw_b, conv_ln_g, conv_ln_b, w_out, ffn2_norm, ffn2_w1,
           ffn2_w3, ffn2_w2):
    b, s, d = x.shape
    assert d == D_MODEL and s % PROJ_ROWS == 0 and (b * s) % FFN_ROWS == 0
    bf = lambda w: w.astype(_BF16)
    row = lambda p: p.reshape(1, -1).astype(_F32)
    for l in range(ffn1_norm.shape[0]):
        x1 = _ffn(x.reshape(b * s, d), row(ffn1_norm[l]), bf(ffn1_w1[l]), bf(ffn1_w3[l]),
                  bf(ffn1_w2[l]))
        q, k, v, u, f = _in_proj(x1.reshape(b, s, d), row(mix_norm[l]), bf(w_in[l]),
                                 row(jnp.tile(q_norm[l], N_HEADS)),
                                 row(jnp.tile(k_norm[l], N_HEADS)))
        conv = _conv(u, conv_dw_w[l].astype(_F32), row(conv_dw_b[l]), row(conv_ln_g[l]),
                     row(conv_ln_b[l]))
        shift = (HEAD_DIM ** 0.5 * jnp.max(jnp.abs(q_norm[l])) * jnp.max(jnp.abs(k_norm[l]))
                 ).astype(_F32).reshape(1, 1)
        attn = _attention(q, f, k, v, shift)
        x = _ffn(x1, row(ffn2_norm[l]), bf(ffn2_w1[l]), bf(ffn2_w3[l]), bf(ffn2_w2[l]),
                 mix=(attn.reshape(b * s, ATTN_WIDTH), conv.reshape(b * s, CONV_WIDTH),
                      bf(w_out[l]))).reshape(b, s, d)
    return x
```

```python
import functools

import numpy as np
import jax
import jax.numpy as jnp
from jax import lax
from jax.experimental import pallas as pl
from jax.experimental.pallas import tpu as pltpu

D_MODEL = 1024
ATTN_WIDTH = 512
CONV_WIDTH = 512
HEAD_DIM = 64
N_HEADS = 8
N_PAIRS = N_HEADS // 2
IN_COLS = 3 * ATTN_WIDTH + 2 * CONV_WIDTH
CONV_K = 31
BLOCK = 256
TOP_K = 3
D_FF = 2816
ALIBI_MAX = 8.0
EPS = 1e-6
MASK_BIAS = -float(2 ** 30)
NEG = -1e30
MIN_ROW_SUM = 2.0 ** -58

LANES = 128
FF_CHUNK = 256
FFN_ROWS = 512
PROJ_ROWS = 512
CONV_ROWS = 256
CONV_HALO = 32
VMEM_LIMIT = 56 * 1024 * 1024

_BF16 = jnp.bfloat16
_F32 = jnp.float32


def _dot(a, b):
    return jnp.dot(a, b, preferred_element_type=_F32)


def _dot_nt(a, b):
    return lax.dot_general(a, b, (((1,), (1,)), ((), ())), preferred_element_type=_F32)


def _split_bf16(x):
    hi = x.astype(_BF16)
    lo = (x - hi.astype(_F32)).astype(_BF16)
    return hi, lo


def _alibi_slopes():
    return np.exp2(-ALIBI_MAX * (np.arange(N_HEADS) + 1.0) / N_HEADS)


def _resident(shape):
    zeros = (0,) * len(shape)
    return pl.BlockSpec(shape, lambda *_: zeros, pipeline_mode=pl.Buffered(1))


def _ffn_tail(x, g_ref, w1_ref, w3_ref, w2_ref, o_ref):
    ms = jnp.mean(x * x, axis=-1, keepdims=True)
    h = (x * lax.rsqrt(ms + EPS) * g_ref[...]).astype(_BF16)
    acc = jnp.zeros(x.shape, _F32)
    for c in range(D_FF // FF_CHUNK):
        sl = slice(c * FF_CHUNK, (c + 1) * FF_CHUNK)
        a = _dot(h, w1_ref[:, sl])
        b = _dot(h, w3_ref[:, sl])
        gl = (a * jax.nn.sigmoid(a) * b).astype(_BF16)
        acc = acc + _dot(gl, w2_ref[sl, :])
    o_ref[...] = x + 0.5 * acc


def _ffn_kernel(x_ref, g_ref, w1_ref, w3_ref, w2_ref, o_ref):
    _ffn_tail(x_ref[...], g_ref, w1_ref, w3_ref, w2_ref, o_ref)


def _mix_ffn_kernel(x_ref, a_ref, c_ref, wo_ref, g_ref, w1_ref, w3_ref, w2_ref, o_ref):
    x = (x_ref[...] + _dot(a_ref[...], wo_ref[:ATTN_WIDTH, :])
         + _dot(c_ref[...], wo_ref[ATTN_WIDTH:, :]))
    _ffn_tail(x, g_ref, w1_ref, w3_ref, w2_ref, o_ref)


def _ffn(x2d, g, w1, w3, w2, mix=None):
    m = x2d.shape[0]
    rows = pl.BlockSpec((FFN_ROWS, D_MODEL), lambda i: (i, 0))
    w_specs = [_resident((1, D_MODEL)), _resident((D_MODEL, D_FF)),
               _resident((D_MODEL, D_FF)), _resident((D_FF, D_MODEL))]
    if mix is None:
        body, ins, specs = _ffn_kernel, (x2d,), [rows]
    else:
        attn, conv, wo = mix
        half = pl.BlockSpec((FFN_ROWS, ATTN_WIDTH), lambda i: (i, 0))
        body, ins = _mix_ffn_kernel, (x2d, attn, conv, wo)
        specs = [rows, half, half, _resident((D_MODEL, D_MODEL))]
    return pl.pallas_call(
        body,
        grid=(m // FFN_ROWS,),
        in_specs=specs + w_specs,
        out_specs=rows,
        out_shape=jax.ShapeDtypeStruct((m, D_MODEL), _F32),
        compiler_params=pltpu.CompilerParams(
            dimension_semantics=("arbitrary",), vmem_limit_bytes=VMEM_LIMIT),
        name="ffn_mix" if mix is not None else "ffn",
    )(*ins, g, w1, w3, w2)


def _head_rms(z, e_ref, gain):
    ms = _dot((z * z).astype(_BF16), e_ref[...])
    return z * lax.rsqrt(ms + EPS) * gain


def _in_proj_kernel(x_ref, g_ref, w_ref, e_ref, gq_ref, gk_ref, hmask_ref, slope_ref,
                    q_ref, k_ref, v_ref, u_ref, f_ref, km_ref):
    t = pl.program_id(1)
    nb = km_ref.shape[0]

    @pl.when(t == 0)
    def _():
        km_ref[...] = jnp.zeros(km_ref.shape, _F32)

    x = x_ref[0]
    ms = jnp.mean(x * x, axis=-1, keepdims=True)
    h = (x * lax.rsqrt(ms + EPS) * g_ref[...]).astype(_BF16)
    a = ATTN_WIDTH
    qn = _head_rms(_dot(h, w_ref[:, 0:a]), e_ref, gq_ref[...])
    kn = _head_rms(_dot(h, w_ref[:, a:2 * a]), e_ref, gk_ref[...])
    q_ref[0] = (qn * (HEAD_DIM ** -0.5)).astype(_BF16)
    k_ref[0] = kn.astype(_BF16)

    n_idx = lax.broadcasted_iota(jnp.int32, (N_HEADS, nb, BLOCK), 1)
    slope_blk = slope_ref[...].reshape(N_HEADS, nb, 1)
    extra_rows = lax.broadcasted_iota(jnp.int32, (LANES - 2 * nb, BLOCK), 0)
    q_off = lax.broadcasted_iota(jnp.int32, (LANES - 2 * nb, BLOCK), 1).astype(_F32)
    slopes = _alibi_slopes()
    extras = [jnp.where(extra_rows < 2, 1.0,
                        jnp.where(extra_rows == 2, -float(slopes[2 * p]) * q_off,
                                  jnp.where(extra_rows == 3, -float(slopes[2 * p + 1]) * q_off, 0.0)))
              for p in range(N_PAIRS)]
    for sb in range(PROJ_ROWS // BLOCK):
        i = t * (PROJ_ROWS // BLOCK) + sb
        rows = slice(sb * BLOCK, (sb + 1) * BLOCK)
        km = km_ref[...]
        kmt = jnp.concatenate([km] * N_HEADS, axis=0) * hmask_ref[...]
        km_hi, km_lo = _split_bf16(kmt)
        q_hi, q_lo = _split_bf16(qn[rows])
        gt = _dot_nt(km_hi, q_hi) + _dot_nt(km_hi, q_lo) + _dot_nt(km_lo, q_hi)
        g3 = gt.reshape(N_HEADS, nb, BLOCK)
        past = n_idx < i
        left = jnp.where(past, g3, -jnp.inf)
        sel = jnp.zeros(g3.shape, jnp.bool_)
        for _ in range(TOP_K):
            best = jnp.max(left, axis=1, keepdims=True)
            first = jnp.min(jnp.where(left == best, n_idx, nb), axis=1, keepdims=True)
            pick = n_idx == first
            sel = sel | pick
            left = jnp.where(pick, -jnp.inf, left)
        sel = sel & past
        alibi_blk = slope_blk * (n_idx - i).astype(_F32)
        bias = jnp.where(sel, alibi_blk, jnp.where(n_idx == i, 0.0, MASK_BIAS))
        bias2 = bias.reshape(N_HEADS * nb, BLOCK)
        for p in range(N_PAIRS):
            ft = jnp.concatenate([bias2[2 * nb * p:2 * nb * (p + 1)], extras[p]], axis=0)
            f_ref[0, p, rows, :] = ft.T.astype(_BF16)
        km_ref[pl.ds(i, 1), :] = jnp.mean(kn[rows], axis=0, keepdims=True)
    v_ref[0] = _dot(h, w_ref[:, 2 * a:3 * a]).astype(_BF16)
    za = _dot(h, w_ref[:, 3 * a:3 * a + CONV_WIDTH])
    zg = _dot(h, w_ref[:, 3 * a + CONV_WIDTH:])
    u_ref[0] = (za * jax.nn.sigmoid(zg)).astype(_BF16)


def _in_proj(x1, g, w_in, gq, gk):
    b, s, _ = x1.shape
    nb = s // BLOCK
    assert 2 * nb + 4 <= LANES
    head_of_col = np.arange(ATTN_WIDTH) // HEAD_DIM
    e = jnp.asarray((head_of_col[:, None] == head_of_col[None, :]) / HEAD_DIM, _BF16)
    hmask = jnp.asarray((np.arange(N_HEADS * nb) // nb)[:, None] == head_of_col[None, :], _F32)
    slope_blk = jnp.asarray(np.repeat(_alibi_slopes() * BLOCK, nb)[:, None], _F32)
    tile = lambda w: pl.BlockSpec((1, PROJ_ROWS, w), lambda bi, ti: (bi, ti, 0))
    act = jax.ShapeDtypeStruct((b, s, ATTN_WIDTH), _BF16)
    return pl.pallas_call(
        _in_proj_kernel,
        grid=(b, s // PROJ_ROWS),
        in_specs=[tile(D_MODEL), _resident((1, D_MODEL)), _resident((D_MODEL, IN_COLS)),
                  _resident((ATTN_WIDTH, ATTN_WIDTH)), _resident((1, ATTN_WIDTH)),
                  _resident((1, ATTN_WIDTH)), _resident((N_HEADS * nb, ATTN_WIDTH)),
                  _resident((N_HEADS * nb, 1))],
        out_specs=[tile(ATTN_WIDTH)] * 4
        + [pl.BlockSpec((1, N_PAIRS, PROJ_ROWS, LANES), lambda bi, ti: (bi, 0, ti, 0))],
        out_shape=[act] * 4 + [jax.ShapeDtypeStruct((b, N_PAIRS, s, LANES), _BF16)],
        scratch_shapes=[pltpu.VMEM((nb, ATTN_WIDTH), _F32)],
        compiler_params=pltpu.CompilerParams(
            dimension_semantics=("arbitrary", "arbitrary"), vmem_limit_bytes=VMEM_LIMIT),
        name="in_proj",
    )(x1, g, w_in, e, gq, gk, hmask, slope_blk)


CONV_RC = 64


def _conv_kernel(u_ref, w_ref, b_ref, lg_ref, lb_ref, o_ref, ubuf_ref, y_ref):
    t = pl.program_id(1)
    n_chunks = CONV_WIDTH // LANES

    @pl.when(t == 0)
    def _():
        ubuf_ref[:, 0:CONV_HALO, :] = jnp.zeros((n_chunks, CONV_HALO, LANES), _F32)

    @pl.when(t > 0)
    def _():
        ubuf_ref[:, 0:CONV_HALO, :] = ubuf_ref[:, CONV_ROWS:CONV_ROWS + CONV_HALO, :]

    for lc in range(n_chunks):
        ln = slice(lc * LANES, (lc + 1) * LANES)
        ubuf_ref[lc, CONV_HALO:, :] = u_ref[0, :, ln].astype(_F32)
        for rc in range(CONV_ROWS // CONV_RC):
            r0 = rc * CONV_RC
            views = {}

            def view(phase, back):
                if (phase, back) not in views:
                    start = r0 + CONV_HALO - 8 * back + phase
                    views[phase, back] = ubuf_ref[lc, pl.ds(start, CONV_RC // 8, stride=8), :]
                return views[phase, back]

            for j in range(8):
                acc = jnp.broadcast_to(b_ref[:, ln], (CONV_RC // 8, LANES))
                for k in range(CONV_K):
                    d = CONV_K - 1 - k
                    acc = acc + w_ref[k:k + 1, ln] * view((j - d) % 8, -((j - d) // 8))
                y_ref[lc, pl.ds(r0 + j, CONV_RC // 8, stride=8), :] = acc
    y = jnp.concatenate([y_ref[lc] for lc in range(n_chunks)], axis=1)
    mu = jnp.mean(y, axis=-1, keepdims=True)
    d = y - mu
    var = jnp.mean(d * d, axis=-1, keepdims=True)
    z = d * lax.rsqrt(var + EPS) * lg_ref[...] + lb_ref[...]
    o_ref[0] = (z * jax.nn.sigmoid(z)).astype(_BF16)


def _conv(u, w, bias, ln_g, ln_b):
    b, s, c = u.shape
    kpad = 8 * (-(-CONV_K // 8))
    tile = pl.BlockSpec((1, CONV_ROWS, c), lambda bi, ti: (bi, ti, 0))
    return pl.pallas_call(
        _conv_kernel,
        grid=(b, s // CONV_ROWS),
        in_specs=[tile, _resident((kpad, c)), _resident((1, c)), _resident((1, c)),
                  _resident((1, c))],
        out_specs=tile,
        out_shape=jax.ShapeDtypeStruct((b, s, c), _BF16),
        scratch_shapes=[pltpu.VMEM((c // LANES, CONV_ROWS + CONV_HALO, LANES), _F32),
                        pltpu.VMEM((c // LANES, CONV_ROWS, LANES), _F32)],
        compiler_params=pltpu.CompilerParams(
            dimension_semantics=("arbitrary", "arbitrary"), vmem_limit_bytes=VMEM_LIMIT),
        name="conv",
    )(u, jnp.pad(w, ((0, kpad - CONV_K), (0, 0))), bias, ln_g, ln_b)


def _stacked_queries(q, f, nb):
    lane = lax.broadcasted_iota(jnp.int32, (BLOCK, LANES), 1)
    zero = jnp.zeros((BLOCK, LANES), _BF16)
    halves = []
    for sl in range(2):
        head_lanes = (lane >= sl * HEAD_DIM) & (lane < (sl + 1) * HEAD_DIM)
        feat_lanes = (((lane >= sl * nb) & (lane < (sl + 1) * nb)) | (lane == 2 * nb + sl)
                      | (lane == 2 * nb + 2 + sl))
        halves.append(jnp.concatenate([jnp.where(head_lanes, q, zero),
                                       jnp.where(feat_lanes, f, zero)], axis=1))
    return jnp.concatenate(halves, axis=0)


def _causal_mask():
    return (lax.broadcasted_iota(jnp.int32, (2 * BLOCK, BLOCK), 1)
            <= lax.broadcasted_iota(jnp.int32, (2 * BLOCK, BLOCK), 0) % BLOCK)


def _merge_heads(acc):
    lane = lax.broadcasted_iota(jnp.int32, (BLOCK, LANES), 1)
    out = acc[:, :LANES] / acc[:, LANES:]
    return jnp.where(lane < HEAD_DIM, out[:BLOCK], out[BLOCK:]).astype(_BF16)


def _attn_row_block(i, nb, shift, q_ref, f_ref, o_ref, kaug_ref, vaug_ref):
    rows = slice(i * BLOCK, (i + 1) * BLOCK)
    qa = _stacked_queries(q_ref[0, rows, :], f_ref[0, 0, rows, :], nb)
    probs = []
    for j in range(i + 1):
        s = _dot_nt(qa, kaug_ref[j * BLOCK:(j + 1) * BLOCK, :])
        if j == i:
            s = jnp.where(_causal_mask(), s, NEG)
        probs.append(jnp.exp(s - shift).astype(_BF16))
    acc = _dot(jnp.concatenate(probs, axis=1), vaug_ref[0:(i + 1) * BLOCK, :])
    o_ref[0, rows, :] = _merge_heads(acc)
    return acc[:, LANES:]


def _attn_row_block_running_max(i, nb, q_ref, f_ref, o_ref, kaug_ref, vaug_ref, acc_ref, m_ref):
    rows = pl.ds(pl.multiple_of(i * BLOCK, BLOCK), BLOCK)
    qa = _stacked_queries(q_ref[0, rows, :], f_ref[0, 0, rows, :], nb)
    m_ref[...] = jnp.full(m_ref.shape, NEG, _F32)
    acc_ref[...] = jnp.zeros(acc_ref.shape, _F32)

    def key_block(j, carry):
        keys = pl.ds(pl.multiple_of(j * BLOCK, BLOCK), BLOCK)
        s = _dot_nt(qa, kaug_ref[keys, :])
        s = jnp.where(_causal_mask() | (j < i), s, NEG)
        m_old = m_ref[...]
        m_new = jnp.maximum(m_old, jnp.max(s, axis=1, keepdims=True))
        m_ref[...] = m_new
        p = jnp.exp(s - jnp.concatenate([m_new, m_new], axis=1)).astype(_BF16)
        rescale = jnp.exp(m_old - m_new)
        acc_ref[...] = (acc_ref[...] * jnp.concatenate([rescale, rescale], axis=1)
                        + _dot(p, vaug_ref[keys, :]))
        return carry

    lax.fori_loop(0, i + 1, key_block, 0)
    o_ref[0, rows, :] = _merge_heads(acc_ref[...])


def _attn_kernel(shift_ref, q_ref, f_ref, k_ref, v_ref, kf_ref, o_ref,
                 kaug_ref, vaug_ref, acc_ref, m_ref):
    nb = kaug_ref.shape[0] // BLOCK
    kaug_ref[:, :LANES] = k_ref[0]
    kaug_ref[:, LANES:] = kf_ref[0]
    vaug_ref[:, :LANES] = v_ref[0]
    vaug_ref[:, LANES:] = jnp.ones((vaug_ref.shape[0], LANES), _BF16)
    shift = shift_ref[0, 0]
    min_sum = None
    for i in range(nb):
        sums = _attn_row_block(i, nb, shift, q_ref, f_ref, o_ref, kaug_ref, vaug_ref)
        min_sum = sums if min_sum is None else jnp.minimum(min_sum, sums)
    healthy = jnp.min(min_sum) >= MIN_ROW_SUM

    @pl.when(jnp.logical_not(healthy))
    def _():
        def row_block(i, carry):
            _attn_row_block_running_max(i, nb, q_ref, f_ref, o_ref, kaug_ref, vaug_ref,
                                        acc_ref, m_ref)
            return carry

        lax.fori_loop(0, nb, row_block, 0)


def _key_features(s):
    nb = s // BLOCK
    slopes = _alibi_slopes()
    pos = np.arange(s)
    feat = np.zeros((N_PAIRS, s, LANES), np.float32)
    onehot = (pos[:, None] // BLOCK == np.arange(nb)[None, :]).astype(np.float32)
    feat[:, :, 0:nb] = onehot
    feat[:, :, nb:2 * nb] = onehot
    for p in range(N_PAIRS):
        for slot in range(2):
            feat[p, :, 2 * nb + slot] = slopes[2 * p + slot] * (pos % BLOCK)
            feat[p, :, 2 * nb + 2 + slot] = 1.0
    assert np.array_equal(feat.astype(_BF16).astype(np.float32), feat)
    return jnp.asarray(feat, _BF16)


def _attention(q, f, k, v, shift):
    b, s, _ = q.shape
    kf = _key_features(s)
    seq = pl.BlockSpec((1, s, LANES), lambda bi, p: (bi, 0, p))
    return pl.pallas_call(
        _attn_kernel,
        grid=(b, N_PAIRS),
        in_specs=[pl.BlockSpec(memory_space=pltpu.SMEM), seq,
                  pl.BlockSpec((1, 1, s, LANES), lambda bi, p: (bi, p, 0, 0)),
                  seq, seq,
                  pl.BlockSpec((1, s, LANES), lambda bi, p: (p, 0, 0))],
        out_specs=seq,
        out_shape=jax.ShapeDtypeStruct((b, s, ATTN_WIDTH), _BF16),
        scratch_shapes=[pltpu.VMEM((s, 2 * LANES), _BF16), pltpu.VMEM((s, 2 * LANES), _BF16),
                        pltpu.VMEM((2 * BLOCK, 2 * LANES), _F32),
                        pltpu.VMEM((2 * BLOCK, LANES), _F32)],
        compiler_params=pltpu.CompilerParams(
            dimension_semantics=("arbitrary", "arbitrary"), vmem_limit_bytes=VMEM_LIMIT),
        name="moba_attention",
    )(shift, q, f, k, v, kf)


def kernel(x, ffn1_norm, ffn1_w1, ffn1_w3, ffn1_w2, mix_norm, w_in, q_norm, k_norm,
           conv_dw_w, conv_dw_b, conv_ln_g, conv_ln_b, w_out, ffn2_norm, ffn2_w1,
           ffn2_w3, ffn2_w2):
    b, s, d = x.shape
    assert d == D_MODEL and s % PROJ_ROWS == 0 and (b * s) % FFN_ROWS == 0
    bf = lambda w: w.astype(_BF16)
    row = lambda p: p.reshape(1, -1).astype(_F32)
    for l in range(ffn1_norm.shape[0]):
        x1 = _ffn(x.reshape(b * s, d), row(ffn1_norm[l]), bf(ffn1_w1[l]), bf(ffn1_w3[l]),
                  bf(ffn1_w2[l]))
        q, k, v, u, f = _in_proj(x1.reshape(b, s, d), row(mix_norm[l]), bf(w_in[l]),
                                 row(jnp.tile(q_norm[l], N_HEADS)),
                                 row(jnp.tile(k_norm[l], N_HEADS)))
        conv = _conv(u, conv_dw_w[l].astype(_F32), row(conv_dw_b[l]), row(conv_ln_g[l]),
                     row(conv_ln_b[l]))
        shift = (HEAD_DIM ** 0.5 * jnp.max(jnp.abs(q_norm[l])) * jnp.max(jnp.abs(k_norm[l]))
                 ).astype(_F32).reshape(1, 1)
        attn = _attention(q, f, k, v, shift)
        x = _ffn(x1, row(ffn2_norm[l]), bf(ffn2_w1[l]), bf(ffn2_w3[l]), bf(ffn2_w2[l]),
                 mix=(attn.reshape(b * s, ATTN_WIDTH), conv.reshape(b * s, CONV_WIDTH),
                      bf(w_out[l]))).reshape(b, s, d)
    return x
```

```python
import functools

import numpy as np
import jax
import jax.numpy as jnp
from jax import lax
from jax.experimental import pallas as pl
from jax.experimental.pallas import tpu as pltpu

D_MODEL = 1024
ATTN_WIDTH = 512
CONV_WIDTH = 512
HEAD_DIM = 64
N_HEADS = 8
N_PAIRS = N_HEADS // 2
IN_COLS = 3 * ATTN_WIDTH + 2 * CONV_WIDTH
CONV_K = 31
BLOCK = 256
TOP_K = 3
D_FF = 2816
ALIBI_MAX = 8.0
EPS = 1e-6
MASK_BIAS = -float(2 ** 30)
NEG = -1e30
MIN_ROW_SUM = 2.0 ** -58

LANES = 128
SUM_ROWS = 16
FF_CHUNK = 256
FFN_ROWS = 512
PROJ_ROWS = 512
CONV_ROWS = 256
CONV_HALO = 32
VMEM_LIMIT = 56 * 1024 * 1024

_BF16 = jnp.bfloat16
_F32 = jnp.float32


def _dot(a, b):
    return jnp.dot(a, b, preferred_element_type=_F32)


def _dot_nt(a, b):
    return lax.dot_general(a, b, (((1,), (1,)), ((), ())), preferred_element_type=_F32)


def _split_bf16(x):
    hi = x.astype(_BF16)
    lo = (x - hi.astype(_F32)).astype(_BF16)
    return hi, lo


def _alibi_slopes():
    return np.exp2(-ALIBI_MAX * (np.arange(N_HEADS) + 1.0) / N_HEADS)


def _resident(shape):
    zeros = (0,) * len(shape)
    return pl.BlockSpec(shape, lambda *_: zeros, pipeline_mode=pl.Buffered(1))


def _ffn_tail(x, g_ref, w1_ref, w3_ref, w2_ref, o_ref):
    ms = jnp.mean(x * x, axis=-1, keepdims=True)
    h = (x * lax.rsqrt(ms + EPS) * g_ref[...]).astype(_BF16)
    acc = jnp.zeros(x.shape, _F32)
    for c in range(D_FF // FF_CHUNK):
        sl = slice(c * FF_CHUNK, (c + 1) * FF_CHUNK)
        a = _dot(h, w1_ref[:, sl])
        b = _dot(h, w3_ref[:, sl])
        gl = (a * jax.nn.sigmoid(a) * b).astype(_BF16)
        acc = acc + _dot(gl, w2_ref[sl, :])
    o_ref[...] = x + 0.5 * acc


def _ffn_kernel(x_ref, g_ref, w1_ref, w3_ref, w2_ref, o_ref):
    _ffn_tail(x_ref[...], g_ref, w1_ref, w3_ref, w2_ref, o_ref)


def _mix_ffn_kernel(x_ref, a_ref, c_ref, wo_ref, g_ref, w1_ref, w3_ref, w2_ref, o_ref):
    x = (x_ref[...] + _dot(a_ref[...], wo_ref[:ATTN_WIDTH, :])
         + _dot(c_ref[...], wo_ref[ATTN_WIDTH:, :]))
    _ffn_tail(x, g_ref, w1_ref, w3_ref, w2_ref, o_ref)


def _ffn(x2d, g, w1, w3, w2, mix=None):
    m = x2d.shape[0]
    rows = pl.BlockSpec((FFN_ROWS, D_MODEL), lambda i: (i, 0))
    w_specs = [_resident((1, D_MODEL)), _resident((D_MODEL, D_FF)),
               _resident((D_MODEL, D_FF)), _resident((D_FF, D_MODEL))]
    if mix is None:
        body, ins, specs = _ffn_kernel, (x2d,), [rows]
    else:
        attn, conv, wo = mix
        half = pl.BlockSpec((FFN_ROWS, ATTN_WIDTH), lambda i: (i, 0))
        body, ins = _mix_ffn_kernel, (x2d, attn, conv, wo)
        specs = [rows, half, half, _resident((D_MODEL, D_MODEL))]
    return pl.pallas_call(
        body,
        grid=(m // FFN_ROWS,),
        in_specs=specs + w_specs,
        out_specs=rows,
        out_shape=jax.ShapeDtypeStruct((m, D_MODEL), _F32),
        compiler_params=pltpu.CompilerParams(
            dimension_semantics=("arbitrary",), vmem_limit_bytes=VMEM_LIMIT),
        name="ffn_mix" if mix is not None else "ffn",
    )(*ins, g, w1, w3, w2)


def _head_rms(z, e_ref, gain):
    ms = _dot((z * z).astype(_BF16), e_ref[...])
    return z * lax.rsqrt(ms + EPS) * gain


def _in_proj_kernel(x_ref, g_ref, w_ref, e_ref, gq_ref, gk_ref, hmask_ref, slope_ref,
                    q_ref, k_ref, v_ref, u_ref, f_ref, km_ref):
    t = pl.program_id(1)
    nb = km_ref.shape[0]

    @pl.when(t == 0)
    def _():
        km_ref[...] = jnp.zeros(km_ref.shape, _F32)

    x = x_ref[0]
    ms = jnp.mean(x * x, axis=-1, keepdims=True)
    h = (x * lax.rsqrt(ms + EPS) * g_ref[...]).astype(_BF16)
    a = ATTN_WIDTH
    qn = _head_rms(_dot(h, w_ref[:, 0:a]), e_ref, gq_ref[...])
    kn = _head_rms(_dot(h, w_ref[:, a:2 * a]), e_ref, gk_ref[...])
    q_ref[0] = (qn * (HEAD_DIM ** -0.5)).astype(_BF16)
    k_ref[0] = kn.astype(_BF16)

    n_idx = lax.broadcasted_iota(jnp.int32, (N_HEADS, nb, BLOCK), 1)
    slope_blk = slope_ref[...].reshape(N_HEADS, nb, 1)
    extra_rows = lax.broadcasted_iota(jnp.int32, (LANES - 2 * nb, BLOCK), 0)
    q_off = lax.broadcasted_iota(jnp.int32, (LANES - 2 * nb, BLOCK), 1).astype(_F32)
    slopes = _alibi_slopes()
    extras = [jnp.where(extra_rows < 2, 1.0,
                        jnp.where(extra_rows == 2, -float(slopes[2 * p]) * q_off,
                                  jnp.where(extra_rows == 3, -float(slopes[2 * p + 1]) * q_off, 0.0)))
              for p in range(N_PAIRS)]
    for sb in range(PROJ_ROWS // BLOCK):
        i = t * (PROJ_ROWS // BLOCK) + sb
        rows = slice(sb * BLOCK, (sb + 1) * BLOCK)
        km = km_ref[...]
        kmt = jnp.concatenate([km] * N_HEADS, axis=0) * hmask_ref[...]
        km_hi, km_lo = _split_bf16(kmt)
        q_hi, q_lo = _split_bf16(qn[rows])
        gt = _dot_nt(km_hi, q_hi) + _dot_nt(km_hi, q_lo) + _dot_nt(km_lo, q_hi)
        g3 = gt.reshape(N_HEADS, nb, BLOCK)
        past = n_idx < i
        left = jnp.where(past, g3, -jnp.inf)
        sel = jnp.zeros(g3.shape, jnp.bool_)
        for _ in range(TOP_K):
            best = jnp.max(left, axis=1, keepdims=True)
            first = jnp.min(jnp.where(left == best, n_idx, nb), axis=1, keepdims=True)
            pick = n_idx == first
            sel = sel | pick
            left = jnp.where(pick, -jnp.inf, left)
        sel = sel & past
        alibi_blk = slope_blk * (n_idx - i).astype(_F32)
        bias = jnp.where(sel, alibi_blk, jnp.where(n_idx == i, 0.0, MASK_BIAS))
        bias2 = bias.reshape(N_HEADS * nb, BLOCK)
        for p in range(N_PAIRS):
            ft = jnp.concatenate([bias2[2 * nb * p:2 * nb * (p + 1)], extras[p]], axis=0)
            f_ref[0, p, rows, :] = ft.T.astype(_BF16)
        km_ref[pl.ds(i, 1), :] = jnp.mean(kn[rows], axis=0, keepdims=True)
    v_ref[0] = _dot(h, w_ref[:, 2 * a:3 * a]).astype(_BF16)
    za = _dot(h, w_ref[:, 3 * a:3 * a + CONV_WIDTH])
    zg = _dot(h, w_ref[:, 3 * a + CONV_WIDTH:])
    u_ref[0] = (za * jax.nn.sigmoid(zg)).astype(_BF16)


def _in_proj(x1, g, w_in, gq, gk):
    b, s, _ = x1.shape
    nb = s // BLOCK
    assert 2 * nb + 4 <= LANES
    head_of_col = np.arange(ATTN_WIDTH) // HEAD_DIM
    e = jnp.asarray((head_of_col[:, None] == head_of_col[None, :]) / HEAD_DIM, _BF16)
    hmask = jnp.asarray((np.arange(N_HEADS * nb) // nb)[:, None] == head_of_col[None, :], _F32)
    slope_blk = jnp.asarray(np.repeat(_alibi_slopes() * BLOCK, nb)[:, None], _F32)
    tile = lambda w: pl.BlockSpec((1, PROJ_ROWS, w), lambda bi, ti: (bi, ti, 0))
    act = jax.ShapeDtypeStruct((b, s, ATTN_WIDTH), _BF16)
    return pl.pallas_call(
        _in_proj_kernel,
        grid=(b, s // PROJ_ROWS),
        in_specs=[tile(D_MODEL), _resident((1, D_MODEL)), _resident((D_MODEL, IN_COLS)),
                  _resident((ATTN_WIDTH, ATTN_WIDTH)), _resident((1, ATTN_WIDTH)),
                  _resident((1, ATTN_WIDTH)), _resident((N_HEADS * nb, ATTN_WIDTH)),
                  _resident((N_HEADS * nb, 1))],
        out_specs=[tile(ATTN_WIDTH)] * 4
        + [pl.BlockSpec((1, N_PAIRS, PROJ_ROWS, LANES), lambda bi, ti: (bi, 0, ti, 0))],
        out_shape=[act] * 4 + [jax.ShapeDtypeStruct((b, N_PAIRS, s, LANES), _BF16)],
        scratch_shapes=[pltpu.VMEM((nb, ATTN_WIDTH), _F32)],
        compiler_params=pltpu.CompilerParams(
            dimension_semantics=("arbitrary", "arbitrary"), vmem_limit_bytes=VMEM_LIMIT),
        name="in_proj",
    )(x1, g, w_in, e, gq, gk, hmask, slope_blk)


CONV_RC = 64


def _conv_kernel(u_ref, w_ref, b_ref, lg_ref, lb_ref, o_ref, ubuf_ref, y_ref):
    t = pl.program_id(1)
    n_chunks = CONV_WIDTH // LANES

    @pl.when(t == 0)
    def _():
        ubuf_ref[:, 0:CONV_HALO, :] = jnp.zeros((n_chunks, CONV_HALO, LANES), _F32)

    @pl.when(t > 0)
    def _():
        ubuf_ref[:, 0:CONV_HALO, :] = ubuf_ref[:, CONV_ROWS:CONV_ROWS + CONV_HALO, :]

    for lc in range(n_chunks):
        ln = slice(lc * LANES, (lc + 1) * LANES)
        ubuf_ref[lc, CONV_HALO:, :] = u_ref[0, :, ln].astype(_F32)
        for rc in range(CONV_ROWS // CONV_RC):
            r0 = rc * CONV_RC
            views = {}

            def view(phase, back):
                if (phase, back) not in views:
                    start = r0 + CONV_HALO - 8 * back + phase
                    views[phase, back] = ubuf_ref[lc, pl.ds(start, CONV_RC // 8, stride=8), :]
                return views[phase, back]

            for j in range(8):
                acc = jnp.broadcast_to(b_ref[:, ln], (CONV_RC // 8, LANES))
                for k in range(CONV_K):
                    d = CONV_K - 1 - k
                    acc = acc + w_ref[k:k + 1, ln] * view((j - d) % 8, -((j - d) // 8))
                y_ref[lc, pl.ds(r0 + j, CONV_RC // 8, stride=8), :] = acc
    y = jnp.concatenate([y_ref[lc] for lc in range(n_chunks)], axis=1)
    mu = jnp.mean(y, axis=-1, keepdims=True)
    d = y - mu
    var = jnp.mean(d * d, axis=-1, keepdims=True)
    z = d * lax.rsqrt(var + EPS) * lg_ref[...] + lb_ref[...]
    o_ref[0] = (z * jax.nn.sigmoid(z)).astype(_BF16)


def _conv(u, w, bias, ln_g, ln_b):
    b, s, c = u.shape
    kpad = 8 * (-(-CONV_K // 8))
    tile = pl.BlockSpec((1, CONV_ROWS, c), lambda bi, ti: (bi, ti, 0))
    return pl.pallas_call(
        _conv_kernel,
        grid=(b, s // CONV_ROWS),
        in_specs=[tile, _resident((kpad, c)), _resident((1, c)), _resident((1, c)),
                  _resident((1, c))],
        out_specs=tile,
        out_shape=jax.ShapeDtypeStruct((b, s, c), _BF16),
        scratch_shapes=[pltpu.VMEM((c // LANES, CONV_ROWS + CONV_HALO, LANES), _F32),
                        pltpu.VMEM((c // LANES, CONV_ROWS, LANES), _F32)],
        compiler_params=pltpu.CompilerParams(
            dimension_semantics=("arbitrary", "arbitrary"), vmem_limit_bytes=VMEM_LIMIT),
        name="conv",
    )(u, jnp.pad(w, ((0, kpad - CONV_K), (0, 0))), bias, ln_g, ln_b)


def _stacked_queries(q, f, nb):
    lane = lax.broadcasted_iota(jnp.int32, (BLOCK, LANES), 1)
    zero = jnp.zeros((BLOCK, LANES), _BF16)
    halves = []
    for sl in range(2):
        head_lanes = (lane >= sl * HEAD_DIM) & (lane < (sl + 1) * HEAD_DIM)
        feat_lanes = (((lane >= sl * nb) & (lane < (sl + 1) * nb)) | (lane == 2 * nb + sl)
                      | (lane == 2 * nb + 2 + sl))
        halves.append(jnp.concatenate([jnp.where(head_lanes, q, zero),
                                       jnp.where(feat_lanes, f, zero)], axis=1))
    return jnp.concatenate(halves, axis=0)


def _causal_mask():
    return (lax.broadcasted_iota(jnp.int32, (2 * BLOCK, BLOCK), 1)
            <= lax.broadcasted_iota(jnp.int32, (2 * BLOCK, BLOCK), 0) % BLOCK)


def _merge_heads(acc):
    lane = lax.broadcasted_iota(jnp.int32, (BLOCK, LANES), 1)
    out = acc[:, :LANES] / acc[:, LANES:]
    return jnp.where(lane < HEAD_DIM, out[:BLOCK], out[BLOCK:]).astype(_BF16)


def _attn_row_block(i, nb, shift, q_ref, f_ref, o_ref, kaug_ref, vaugt_ref):
    rows = slice(i * BLOCK, (i + 1) * BLOCK)
    n = (i + 1) * BLOCK
    qa = _stacked_queries(q_ref[0, rows, :], f_ref[0, 0, rows, :], nb)
    qat = qa.astype(_F32).T.astype(_BF16)
    st = _dot(kaug_ref[0:n, :], qat)
    key = lax.broadcasted_iota(jnp.int32, (BLOCK, 2 * BLOCK), 0)
    qry = lax.broadcasted_iota(jnp.int32, (BLOCK, 2 * BLOCK), 1) % BLOCK
    probs = []
    for j in range(i + 1):
        s = st[j * BLOCK:(j + 1) * BLOCK]
        if j == i:
            s = jnp.where(key <= qry, s, NEG)
        probs.append(jnp.exp(s - shift).astype(_BF16))
    acc = _dot(vaugt_ref[:, 0:n], jnp.concatenate(probs, axis=0))
    sums = acc[LANES:LANES + 1, :]
    out = acc[:LANES, :] / sums
    merged = jnp.concatenate([out[:HEAD_DIM, :BLOCK], out[HEAD_DIM:, BLOCK:]], axis=0)
    o_ref[0, rows, :] = merged.T.astype(_BF16)
    return sums


def _attn_row_block_running_max(i, nb, q_ref, f_ref, o_ref, kaug_ref, vaug_ref, acc_ref, m_ref):
    rows = pl.ds(pl.multiple_of(i * BLOCK, BLOCK), BLOCK)
    qa = _stacked_queries(q_ref[0, rows, :], f_ref[0, 0, rows, :], nb)
    m_ref[...] = jnp.full(m_ref.shape, NEG, _F32)
    acc_ref[...] = jnp.zeros(acc_ref.shape, _F32)

    def key_block(j, carry):
        keys = pl.ds(pl.multiple_of(j * BLOCK, BLOCK), BLOCK)
        s = _dot_nt(qa, kaug_ref[keys, :])
        s = jnp.where(_causal_mask() | (j < i), s, NEG)
        m_old = m_ref[...]
        m_new = jnp.maximum(m_old, jnp.max(s, axis=1, keepdims=True))
        m_ref[...] = m_new
        p = jnp.exp(s - jnp.concatenate([m_new, m_new], axis=1)).astype(_BF16)
        rescale = jnp.exp(m_old - m_new)
        acc_ref[...] = (acc_ref[...] * jnp.concatenate([rescale, rescale], axis=1)
                        + _dot(p, vaug_ref[keys, :]))
        return carry

    lax.fori_loop(0, i + 1, key_block, 0)
    o_ref[0, rows, :] = _merge_heads(acc_ref[...])


def _attn_kernel(shift_ref, q_ref, f_ref, k_ref, v_ref, kf_ref, o_ref,
                 kaug_ref, vaugt_ref, vaug_ref, acc_ref, m_ref):
    nb = kaug_ref.shape[0] // BLOCK
    kaug_ref[:, :LANES] = k_ref[0]
    kaug_ref[:, LANES:] = kf_ref[0]
    vaugt_ref[:LANES, :] = v_ref[0].astype(_F32).T.astype(_BF16)
    vaugt_ref[LANES:, :] = jnp.ones((SUM_ROWS, vaugt_ref.shape[1]), _BF16)
    shift = shift_ref[0, 0]
    min_sum = None
    for i in range(nb):
        sums = _attn_row_block(i, nb, shift, q_ref, f_ref, o_ref, kaug_ref, vaugt_ref)
        min_sum = sums if min_sum is None else jnp.minimum(min_sum, sums)
    healthy = jnp.min(min_sum) >= MIN_ROW_SUM

    @pl.when(jnp.logical_not(healthy))
    def _():
        vaug_ref[:, :LANES] = v_ref[0]
        vaug_ref[:, LANES:] = jnp.ones((vaug_ref.shape[0], LANES), _BF16)

        def row_block(i, carry):
            _attn_row_block_running_max(i, nb, q_ref, f_ref, o_ref, kaug_ref, vaug_ref,
                                        acc_ref, m_ref)
            return carry

        lax.fori_loop(0, nb, row_block, 0)


def _key_features(s):
    nb = s // BLOCK
    slopes = _alibi_slopes()
    pos = np.arange(s)
    feat = np.zeros((N_PAIRS, s, LANES), np.float32)
    onehot = (pos[:, None] // BLOCK == np.arange(nb)[None, :]).astype(np.float32)
    feat[:, :, 0:nb] = onehot
    feat[:, :, nb:2 * nb] = onehot
    for p in range(N_PAIRS):
        for slot in range(2):
            feat[p, :, 2 * nb + slot] = slopes[2 * p + slot] * (pos % BLOCK)
            feat[p, :, 2 * nb + 2 + slot] = 1.0
    assert np.array_equal(feat.astype(_BF16).astype(np.float32), feat)
    return jnp.asarray(feat, _BF16)


def _attention(q, f, k, v, shift):
    b, s, _ = q.shape
    kf = _key_features(s)
    seq = pl.BlockSpec((1, s, LANES), lambda bi, p: (bi, 0, p))
    return pl.pallas_call(
        _attn_kernel,
        grid=(b, N_PAIRS),
        in_specs=[pl.BlockSpec(memory_space=pltpu.SMEM), seq,
                  pl.BlockSpec((1, 1, s, LANES), lambda bi, p: (bi, p, 0, 0)),
                  seq, seq,
                  pl.BlockSpec((1, s, LANES), lambda bi, p: (p, 0, 0))],
        out_specs=seq,
        out_shape=jax.ShapeDtypeStruct((b, s, ATTN_WIDTH), _BF16),
        scratch_shapes=[pltpu.VMEM((s, 2 * LANES), _BF16),
                        pltpu.VMEM((LANES + SUM_ROWS, s), _BF16),
                        pltpu.VMEM((s, 2 * LANES), _BF16),
                        pltpu.VMEM((2 * BLOCK, 2 * LANES), _F32),
                        pltpu.VMEM((2 * BLOCK, LANES), _F32)],
        compiler_params=pltpu.CompilerParams(
            dimension_semantics=("arbitrary", "arbitrary"), vmem_limit_bytes=VMEM_LIMIT),
        name="moba_attention",
    )(shift, q, f, k, v, kf)


def kernel(x, ffn1_norm, ffn1_w1, ffn1_w3, ffn1_w2, mix_norm, w_in, q_norm, k_norm,
           conv_dw_w, conv_dw_b, conv_ln_g, conv_ln_b, w_out, ffn2_norm, ffn2_w1,
           ffn2_w3, ffn2_w2):
    b, s, d = x.shape
    assert d == D_MODEL and s % PROJ_ROWS == 0 and (b * s) % FFN_ROWS == 0
    bf = lambda w: w.astype(_BF16)
    row = lambda p: p.reshape(1, -1).astype(_F32)
    for l in range(ffn1_norm.shape[0]):
        x1 = _ffn(x.reshape(b * s, d), row(ffn1_norm[l]), bf(ffn1_w1[l]), bf(ffn1_w3[l]),
                  bf(ffn1_w2[l]))
        q, k, v, u, f = _in_proj(x1.reshape(b, s, d), row(mix_norm[l]), bf(w_in[l]),
                                 row(jnp.tile(q_norm[l], N_HEADS)),
                                 row(jnp.tile(k_norm[l], N_HEADS)))
        conv = _conv(u, conv_dw_w[l].astype(_F32), row(conv_dw_b[l]), row(conv_ln_g[l]),
                     row(conv_ln_b[l]))
        shift = (HEAD_DIM ** 0.5 * jnp.max(jnp.abs(q_norm[l])) * jnp.max(jnp.abs(k_norm[l]))
                 ).astype(_F32).reshape(1, 1)
        attn = _attention(q, f, k, v, shift)
        x = _ffn(x1, row(ffn2_norm[l]), bf(ffn2_w1[l]), bf(ffn2_w3[l]), bf(ffn2_w2[l]),
                 mix=(attn.reshape(b * s, ATTN_WIDTH), conv.reshape(b * s, CONV_WIDTH),
                      bf(w_out[l]))).reshape(b, s, d)
    return x
```

```python
import functools

import numpy as np
import jax
import jax.numpy as jnp
from jax import lax
from jax.experimental import pallas as pl
from jax.experimental.pallas import tpu as pltpu

D_MODEL = 1024
ATTN_WIDTH = 512
CONV_WIDTH = 512
HEAD_DIM = 64
N_HEADS = 8
N_PAIRS = N_HEADS // 2
IN_COLS = 3 * ATTN_WIDTH + 2 * CONV_WIDTH
CONV_K = 31
BLOCK = 256
TOP_K = 3
D_FF = 2816
ALIBI_MAX = 8.0
EPS = 1e-6
MASK_BIAS = -float(2 ** 30)
NEG = -1e30
MIN_ROW_SUM = 2.0 ** -58

LANES = 128
SUM_ROWS = 16
FF_CHUNK = 256
FFN_ROWS = 512
PROJ_ROWS = 1024
CONV_HALO = 32
VMEM_LIMIT = 56 * 1024 * 1024

_BF16 = jnp.bfloat16
_F32 = jnp.float32


def _dot(a, b):
    return jnp.dot(a, b, preferred_element_type=_F32)


def _dot_nt(a, b):
    return lax.dot_general(a, b, (((1,), (1,)), ((), ())), preferred_element_type=_F32)


def _split_bf16(x):
    hi = x.astype(_BF16)
    lo = (x - hi.astype(_F32)).astype(_BF16)
    return hi, lo


def _alibi_slopes():
    return np.exp2(-ALIBI_MAX * (np.arange(N_HEADS) + 1.0) / N_HEADS)


def _resident(shape):
    zeros = (0,) * len(shape)
    return pl.BlockSpec(shape, lambda *_: zeros, pipeline_mode=pl.Buffered(1))


def _ffn_tail(x, g_ref, w1_ref, w3_ref, w2_ref, o_ref, side_work=()):
    ms = jnp.mean(x * x, axis=-1, keepdims=True)
    h = (x * lax.rsqrt(ms + EPS) * g_ref[...]).astype(_BF16)
    acc = jnp.zeros(x.shape, _F32)
    n_chunks = D_FF // FF_CHUNK
    per_chunk = -(-len(side_work) // (n_chunks - 1))
    done = None
    for c in range(n_chunks):
        sl = slice(c * FF_CHUNK, (c + 1) * FF_CHUNK)
        a = _dot(h, w1_ref[:, sl])
        if done is not None:
            a = jnp.concatenate([jnp.concatenate([a[:8, :LANES] + done, a[:8, LANES:]], axis=1),
                                 a[8:]], axis=0)
        b = _dot(h, w3_ref[:, sl])
        gl = (a * jax.nn.sigmoid(a) * b).astype(_BF16)
        acc = acc + _dot(gl, w2_ref[sl, :])
        for piece in side_work[c * per_chunk:(c + 1) * per_chunk]:
            zero = jnp.where(piece() > jnp.inf, 1.0, 0.0)
            done = zero if done is None else done + zero
    o_ref[...] = x + 0.5 * acc


def _ffn_kernel(x_ref, g_ref, w1_ref, w3_ref, w2_ref, o_ref):
    _ffn_tail(x_ref[...], g_ref, w1_ref, w3_ref, w2_ref, o_ref)


def _mix_ffn_kernel(tiles_per_seq, x_ref, a_ref, u0_ref, un_ref, wo_ref, cw_ref, cb_ref, clg_ref,
                    clb_ref, g_ref, w1_ref, w3_ref, w2_ref, o_ref, ubuf_ref, y_ref, conv_ref):
    t = pl.program_id(0)
    cur = t % 2
    params = (cw_ref, cb_ref, clg_ref, clb_ref)

    @pl.when(t == 0)
    def _():
        for piece in _conv_tile_pieces(u0_ref, True, ubuf_ref, y_ref, conv_ref, 0, *params):
            piece()

    ahead = _conv_tile_pieces(un_ref, (t + 1) % tiles_per_seq == 0, ubuf_ref, y_ref, conv_ref,
                              1 - cur, *params)
    x = (x_ref[...] + _dot(a_ref[...], wo_ref[:ATTN_WIDTH, :])
         + _dot(conv_ref[cur], wo_ref[ATTN_WIDTH:, :]))
    _ffn_tail(x, g_ref, w1_ref, w3_ref, w2_ref, o_ref, side_work=ahead)


def _ffn(x2d, g, w1, w3, w2, mix=None):
    m = x2d.shape[0]
    rows = pl.BlockSpec((FFN_ROWS, D_MODEL), lambda i: (i, 0))
    w_specs = [_resident((1, D_MODEL)), _resident((D_MODEL, D_FF)),
               _resident((D_MODEL, D_FF)), _resident((D_FF, D_MODEL))]
    scratch = []
    if mix is None:
        body, ins, specs = _ffn_kernel, (x2d,), [rows]
    else:
        attn, u, wo, conv_w, conv_b, ln_g, ln_b, seq_len = mix
        last = m // FFN_ROWS - 1
        half = lambda index: pl.BlockSpec((FFN_ROWS, ATTN_WIDTH), index)
        kpad = 8 * (-(-CONV_K // 8))
        body = functools.partial(_mix_ffn_kernel, seq_len // FFN_ROWS)
        ins = (x2d, attn, u, u, wo, jnp.pad(conv_w, ((0, kpad - CONV_K), (0, 0))), conv_b, ln_g,
               ln_b)
        specs = [rows, half(lambda i: (i, 0)), half(lambda i: (0, 0)),
                 half(lambda i: (jnp.minimum(i + 1, last), 0)), _resident((D_MODEL, D_MODEL)),
                 _resident((kpad, CONV_WIDTH))] + [_resident((1, CONV_WIDTH))] * 3
        n_chunks = CONV_WIDTH // LANES
        scratch = [pltpu.VMEM((n_chunks, FFN_ROWS + CONV_HALO, LANES), _F32),
                   pltpu.VMEM((n_chunks, FFN_ROWS, LANES), _F32),
                   pltpu.VMEM((2, FFN_ROWS, CONV_WIDTH), _BF16)]
    return pl.pallas_call(
        body,
        grid=(m // FFN_ROWS,),
        in_specs=specs + w_specs,
        out_specs=rows,
        out_shape=jax.ShapeDtypeStruct((m, D_MODEL), _F32),
        scratch_shapes=scratch,
        compiler_params=pltpu.CompilerParams(
            dimension_semantics=("arbitrary",), vmem_limit_bytes=VMEM_LIMIT),
        name="ffn_mix" if mix is not None else "ffn",
    )(*ins, g, w1, w3, w2)


def _head_rms(z, e_ref, gain):
    ms = _dot((z * z).astype(_BF16), e_ref[...])
    return z * lax.rsqrt(ms + EPS) * gain


def _in_proj_kernel(x_ref, g_ref, w_ref, e_ref, gq_ref, gk_ref, hmask_ref, slope_ref,
                    q_ref, k_ref, v_ref, u_ref, f_ref, km_ref):
    t = pl.program_id(1)
    nb = km_ref.shape[0]

    @pl.when(t == 0)
    def _():
        km_ref[...] = jnp.zeros(km_ref.shape, _F32)

    x = x_ref[0]
    ms = jnp.mean(x * x, axis=-1, keepdims=True)
    h = (x * lax.rsqrt(ms + EPS) * g_ref[...]).astype(_BF16)
    a = ATTN_WIDTH
    qn = _head_rms(_dot(h, w_ref[:, 0:a]), e_ref, gq_ref[...])
    kn = _head_rms(_dot(h, w_ref[:, a:2 * a]), e_ref, gk_ref[...])
    q_ref[0] = (qn * (HEAD_DIM ** -0.5)).astype(_BF16)
    k_ref[0] = kn.astype(_BF16)

    n_idx = lax.broadcasted_iota(jnp.int32, (N_HEADS, nb, BLOCK), 1)
    slope_blk = slope_ref[...].reshape(N_HEADS, nb, 1)
    extra_rows = lax.broadcasted_iota(jnp.int32, (LANES - 2 * nb, BLOCK), 0)
    q_off = lax.broadcasted_iota(jnp.int32, (LANES - 2 * nb, BLOCK), 1).astype(_F32)
    slopes = _alibi_slopes()
    extras = [jnp.where(extra_rows < 2, 1.0,
                        jnp.where(extra_rows == 2, -float(slopes[2 * p]) * q_off,
                                  jnp.where(extra_rows == 3, -float(slopes[2 * p + 1]) * q_off, 0.0)))
              for p in range(N_PAIRS)]
    for sb in range(PROJ_ROWS // BLOCK):
        i = t * (PROJ_ROWS // BLOCK) + sb
        rows = slice(sb * BLOCK, (sb + 1) * BLOCK)
        km = km_ref[...]
        kmt = jnp.concatenate([km] * N_HEADS, axis=0) * hmask_ref[...]
        km_hi, km_lo = _split_bf16(kmt)
        q_hi, q_lo = _split_bf16(qn[rows])
        gt = _dot_nt(km_hi, q_hi) + _dot_nt(km_hi, q_lo) + _dot_nt(km_lo, q_hi)
        g3 = gt.reshape(N_HEADS, nb, BLOCK)
        past = n_idx < i
        left = jnp.where(past, g3, -jnp.inf)
        sel = jnp.zeros(g3.shape, jnp.bool_)
        for _ in range(TOP_K):
            best = jnp.max(left, axis=1, keepdims=True)
            first = jnp.min(jnp.where(left == best, n_idx, nb), axis=1, keepdims=True)
            pick = n_idx == first
            sel = sel | pick
            left = jnp.where(pick, -jnp.inf, left)
        sel = sel & past
        alibi_blk = slope_blk * (n_idx - i).astype(_F32)
        bias = jnp.where(sel, alibi_blk, jnp.where(n_idx == i, 0.0, MASK_BIAS))
        bias2 = bias.reshape(N_HEADS * nb, BLOCK)
        for p in range(N_PAIRS):
            ft = jnp.concatenate([bias2[2 * nb * p:2 * nb * (p + 1)], extras[p]], axis=0)
            f_ref[0, p, rows, :] = ft.T.astype(_BF16)
        km_ref[pl.ds(i, 1), :] = jnp.mean(kn[rows], axis=0, keepdims=True)
    v_ref[0] = _dot(h, w_ref[:, 2 * a:3 * a]).astype(_BF16)
    za = _dot(h, w_ref[:, 3 * a:3 * a + CONV_WIDTH])
    zg = _dot(h, w_ref[:, 3 * a + CONV_WIDTH:])
    u_ref[0] = (za * jax.nn.sigmoid(zg)).astype(_BF16)


def _in_proj(x1, g, w_in, gq, gk):
    b, s, _ = x1.shape
    nb = s // BLOCK
    assert 2 * nb + 4 <= LANES
    head_of_col = np.arange(ATTN_WIDTH) // HEAD_DIM
    e = jnp.asarray((head_of_col[:, None] == head_of_col[None, :]) / HEAD_DIM, _BF16)
    hmask = jnp.asarray((np.arange(N_HEADS * nb) // nb)[:, None] == head_of_col[None, :], _F32)
    slope_blk = jnp.asarray(np.repeat(_alibi_slopes() * BLOCK, nb)[:, None], _F32)
    tile = lambda w: pl.BlockSpec((1, PROJ_ROWS, w), lambda bi, ti: (bi, ti, 0))
    act = jax.ShapeDtypeStruct((b, s, ATTN_WIDTH), _BF16)
    return pl.pallas_call(
        _in_proj_kernel,
        grid=(b, s // PROJ_ROWS),
        in_specs=[tile(D_MODEL), _resident((1, D_MODEL)), _resident((D_MODEL, IN_COLS)),
                  _resident((ATTN_WIDTH, ATTN_WIDTH)), _resident((1, ATTN_WIDTH)),
                  _resident((1, ATTN_WIDTH)), _resident((N_HEADS * nb, ATTN_WIDTH)),
                  _resident((N_HEADS * nb, 1))],
        out_specs=[tile(ATTN_WIDTH)] * 4
        + [pl.BlockSpec((1, N_PAIRS, PROJ_ROWS, LANES), lambda bi, ti: (bi, 0, ti, 0))],
        out_shape=[act] * 4 + [jax.ShapeDtypeStruct((b, N_PAIRS, s, LANES), _BF16)],
        scratch_shapes=[pltpu.VMEM((nb, ATTN_WIDTH), _F32)],
        compiler_params=pltpu.CompilerParams(
            dimension_semantics=("arbitrary", "arbitrary"), vmem_limit_bytes=VMEM_LIMIT),
        name="in_proj",
    )(x1, g, w_in, e, gq, gk, hmask, slope_blk)


CONV_RC = 64


def _conv_tile_pieces(u_ref, seq_start, ubuf_ref, y_ref, out_ref, out_slot,
                      w_ref, b_ref, lg_ref, lb_ref):
    n_chunks = CONV_WIDTH // LANES
    pieces = []

    def load(lc):
        ln = slice(lc * LANES, (lc + 1) * LANES)
        tail = ubuf_ref[lc, FFN_ROWS:FFN_ROWS + CONV_HALO, :]
        ubuf_ref[lc, 0:CONV_HALO, :] = jnp.where(seq_start, 0.0, tail)
        ubuf_ref[lc, CONV_HALO:, :] = u_ref[:, ln].astype(_F32)
        return tail[:8]

    def taps(lc, r0):
        ln = slice(lc * LANES, (lc + 1) * LANES)
        views = {}

        def view(phase, back):
            if (phase, back) not in views:
                start = r0 + CONV_HALO - 8 * back + phase
                views[phase, back] = ubuf_ref[lc, pl.ds(start, CONV_RC // 8, stride=8), :]
            return views[phase, back]

        for j in range(8):
            acc = jnp.broadcast_to(b_ref[:, ln], (CONV_RC // 8, LANES))
            for k in range(CONV_K):
                d = CONV_K - 1 - k
                acc = acc + w_ref[k:k + 1, ln] * view((j - d) % 8, -((j - d) // 8))
            y_ref[lc, pl.ds(r0 + j, CONV_RC // 8, stride=8), :] = acc
        return acc

    def finish(r0):
        rows = slice(r0, r0 + FFN_ROWS // 2)
        y = jnp.concatenate([y_ref[lc, rows, :] for lc in range(n_chunks)], axis=1)
        mu = jnp.mean(y, axis=-1, keepdims=True)
        d = y - mu
        var = jnp.mean(d * d, axis=-1, keepdims=True)
        z = d * lax.rsqrt(var + EPS) * lg_ref[...] + lb_ref[...]
        act = z * jax.nn.sigmoid(z)
        out_ref[out_slot, rows, :] = act.astype(_BF16)
        return act[:8, :LANES]

    for lc in range(n_chunks):
        pieces.append(functools.partial(load, lc))
        for r0 in range(0, FFN_ROWS, CONV_RC):
            pieces.append(functools.partial(taps, lc, r0))
    pieces.append(functools.partial(finish, 0))
    pieces.append(functools.partial(finish, FFN_ROWS // 2))
    return pieces


def _stacked_queries(q, f, nb):
    lane = lax.broadcasted_iota(jnp.int32, (BLOCK, LANES), 1)
    zero = jnp.zeros((BLOCK, LANES), _BF16)
    halves = []
    for sl in range(2):
        head_lanes = (lane >= sl * HEAD_DIM) & (lane < (sl + 1) * HEAD_DIM)
        feat_lanes = (((lane >= sl * nb) & (lane < (sl + 1) * nb)) | (lane == 2 * nb + sl)
                      | (lane == 2 * nb + 2 + sl))
        halves.append(jnp.concatenate([jnp.where(head_lanes, q, zero),
                                       jnp.where(feat_lanes, f, zero)], axis=1))
    return jnp.concatenate(halves, axis=0)


def _causal_mask():
    return (lax.broadcasted_iota(jnp.int32, (2 * BLOCK, BLOCK), 1)
            <= lax.broadcasted_iota(jnp.int32, (2 * BLOCK, BLOCK), 0) % BLOCK)


def _merge_heads(acc):
    lane = lax.broadcasted_iota(jnp.int32, (BLOCK, LANES), 1)
    out = acc[:, :LANES] / acc[:, LANES:]
    return jnp.where(lane < HEAD_DIM, out[:BLOCK], out[BLOCK:]).astype(_BF16)


def _attn_row_block(i, nb, shift, q_ref, f_ref, o_ref, kaug_ref, vaugt_ref):
    rows = slice(i * BLOCK, (i + 1) * BLOCK)
    n = (i + 1) * BLOCK
    qa = _stacked_queries(q_ref[0, rows, :], f_ref[0, 0, rows, :], nb)
    qat = qa.astype(_F32).T.astype(_BF16)
    st = _dot(kaug_ref[0:n, :], qat)
    key = lax.broadcasted_iota(jnp.int32, (BLOCK, 2 * BLOCK), 0)
    qry = lax.broadcasted_iota(jnp.int32, (BLOCK, 2 * BLOCK), 1) % BLOCK
    probs = []
    for j in range(i + 1):
        s = st[j * BLOCK:(j + 1) * BLOCK]
        if j == i:
            s = jnp.where(key <= qry, s, NEG)
        probs.append(jnp.exp(s - shift).astype(_BF16))
    acc = _dot(vaugt_ref[:, 0:n], jnp.concatenate(probs, axis=0))
    sums = acc[LANES:LANES + 1, :]
    out = acc[:LANES, :] / sums
    merged = jnp.concatenate([out[:HEAD_DIM, :BLOCK], out[HEAD_DIM:, BLOCK:]], axis=0)
    o_ref[0, rows, :] = merged.T.astype(_BF16)
    return sums


def _attn_row_block_running_max(i, nb, q_ref, f_ref, o_ref, kaug_ref, vaug_ref, acc_ref, m_ref):
    rows = pl.ds(pl.multiple_of(i * BLOCK, BLOCK), BLOCK)
    qa = _stacked_queries(q_ref[0, rows, :], f_ref[0, 0, rows, :], nb)
    m_ref[...] = jnp.full(m_ref.shape, NEG, _F32)
    acc_ref[...] = jnp.zeros(acc_ref.shape, _F32)

    def key_block(j, carry):
        keys = pl.ds(pl.multiple_of(j * BLOCK, BLOCK), BLOCK)
        s = _dot_nt(qa, kaug_ref[keys, :])
        s = jnp.where(_causal_mask() | (j < i), s, NEG)
        m_old = m_ref[...]
        m_new = jnp.maximum(m_old, jnp.max(s, axis=1, keepdims=True))
        m_ref[...] = m_new
        p = jnp.exp(s - jnp.concatenate([m_new, m_new], axis=1)).astype(_BF16)
        rescale = jnp.exp(m_old - m_new)
        acc_ref[...] = (acc_ref[...] * jnp.concatenate([rescale, rescale], axis=1)
                        + _dot(p, vaug_ref[keys, :]))
        return carry

    lax.fori_loop(0, i + 1, key_block, 0)
    o_ref[0, rows, :] = _merge_heads(acc_ref[...])


def _attn_kernel(shift_ref, q_ref, f_ref, k_ref, v_ref, kf_ref, o_ref,
                 kaug_ref, vaugt_ref, vaug_ref, acc_ref, m_ref):
    nb = kaug_ref.shape[0] // BLOCK
    kaug_ref[:, :LANES] = k_ref[0]
    kaug_ref[:, LANES:] = kf_ref[0]
    vaugt_ref[:LANES, :] = v_ref[0].astype(_F32).T.astype(_BF16)
    vaugt_ref[LANES:, :] = jnp.ones((SUM_ROWS, vaugt_ref.shape[1]), _BF16)
    shift = shift_ref[0, 0]
    min_sum = None
    for i in range(nb):
        sums = _attn_row_block(i, nb, shift, q_ref, f_ref, o_ref, kaug_ref, vaugt_ref)
        min_sum = sums if min_sum is None else jnp.minimum(min_sum, sums)
    healthy = jnp.min(min_sum) >= MIN_ROW_SUM

    @pl.when(jnp.logical_not(healthy))
    def _():
        vaug_ref[:, :LANES] = v_ref[0]
        vaug_ref[:, LANES:] = jnp.ones((vaug_ref.shape[0], LANES), _BF16)

        def row_block(i, carry):
            _attn_row_block_running_max(i, nb, q_ref, f_ref, o_ref, kaug_ref, vaug_ref,
                                        acc_ref, m_ref)
            return carry

        lax.fori_loop(0, nb, row_block, 0)


def _key_features(s):
    nb = s // BLOCK
    slopes = _alibi_slopes()
    pos = np.arange(s)
    feat = np.zeros((N_PAIRS, s, LANES), np.float32)
    onehot = (pos[:, None] // BLOCK == np.arange(nb)[None, :]).astype(np.float32)
    feat[:, :, 0:nb] = onehot
    feat[:, :, nb:2 * nb] = onehot
    for p in range(N_PAIRS):
        for slot in range(2):
            feat[p, :, 2 * nb + slot] = slopes[2 * p + slot] * (pos % BLOCK)
            feat[p, :, 2 * nb + 2 + slot] = 1.0
    assert np.array_equal(feat.astype(_BF16).astype(np.float32), feat)
    return jnp.asarray(feat, _BF16)


def _attention(q, f, k, v, shift):
    b, s, _ = q.shape
    kf = _key_features(s)
    seq = pl.BlockSpec((1, s, LANES), lambda bi, p: (bi, 0, p))
    return pl.pallas_call(
        _attn_kernel,
        grid=(b, N_PAIRS),
        in_specs=[pl.BlockSpec(memory_space=pltpu.SMEM), seq,
                  pl.BlockSpec((1, 1, s, LANES), lambda bi, p: (bi, p, 0, 0)),
                  seq, seq,
                  pl.BlockSpec((1, s, LANES), lambda bi, p: (p, 0, 0))],
        out_specs=seq,
        out_shape=jax.ShapeDtypeStruct((b, s, ATTN_WIDTH), _BF16),
        scratch_shapes=[pltpu.VMEM((s, 2 * LANES), _BF16),
                        pltpu.VMEM((LANES + SUM_ROWS, s), _BF16),
                        pltpu.VMEM((s, 2 * LANES), _BF16),
                        pltpu.VMEM((2 * BLOCK, 2 * LANES), _F32),
                        pltpu.VMEM((2 * BLOCK, LANES), _F32)],
        compiler_params=pltpu.CompilerParams(
            dimension_semantics=("arbitrary", "arbitrary"), vmem_limit_bytes=VMEM_LIMIT),
        name="moba_attention",
    )(shift, q, f, k, v, kf)


def kernel(x, ffn1_norm, ffn1_w1, ffn1_w3, ffn1_w2, mix_norm, w_in, q_norm, k_norm,
           conv_dw_w, conv_dw_b, conv_ln_g, conv_ln_b, w_out, ffn2_norm, ffn2_w1,
           ffn2_w3, ffn2_w2):
    b, s, d = x.shape
    assert d == D_MODEL and s % PROJ_ROWS == 0 and s % FFN_ROWS == 0
    bf = lambda w: w.astype(_BF16)
    row = lambda p: p.reshape(1, -1).astype(_F32)
    for l in range(ffn1_norm.shape[0]):
        x1 = _ffn(x.reshape(b * s, d), row(ffn1_norm[l]), bf(ffn1_w1[l]), bf(ffn1_w3[l]),
                  bf(ffn1_w2[l]))
        q, k, v, u, f = _in_proj(x1.reshape(b, s, d), row(mix_norm[l]), bf(w_in[l]),
                                 row(jnp.tile(q_norm[l], N_HEADS)),
                                 row(jnp.tile(k_norm[l], N_HEADS)))
        shift = (HEAD_DIM ** 0.5 * jnp.max(jnp.abs(q_norm[l])) * jnp.max(jnp.abs(k_norm[l]))
                 ).astype(_F32).reshape(1, 1)
        attn = _attention(q, f, k, v, shift)
        x = _ffn(x1, row(ffn2_norm[l]), bf(ffn2_w1[l]), bf(ffn2_w3[l]), bf(ffn2_w2[l]),
                 mix=(attn.reshape(b * s, ATTN_WIDTH), u.reshape(b * s, CONV_WIDTH), bf(w_out[l]),
                      conv_dw_w[l].astype(_F32), row(conv_dw_b[l]), row(conv_ln_g[l]),
                      row(conv_ln_b[l]), s)).reshape(b, s, d)
    return x
```

```python
import functools

import numpy as np
import jax
import jax.numpy as jnp
from jax import lax
from jax.experimental import pallas as pl
from jax.experimental.pallas import tpu as pltpu

D_MODEL = 1024
ATTN_WIDTH = 512
CONV_WIDTH = 512
HEAD_DIM = 64
N_HEADS = 8
N_PAIRS = N_HEADS // 2
IN_COLS = 3 * ATTN_WIDTH + 2 * CONV_WIDTH
CONV_K = 31
BLOCK = 256
TOP_K = 3
D_FF = 2816
ALIBI_MAX = 8.0
EPS = 1e-6
MASK_BIAS = -float(2 ** 30)
NEG = -1e30
MIN_ROW_SUM = 2.0 ** -58

LANES = 128
SUM_ROWS = 16
FF_CHUNK = 256
FFN_ROWS = 512
PROJ_ROWS = 1024
CONV_ROWS = 256
CONV_HALO = 32
VMEM_LIMIT = 56 * 1024 * 1024

_BF16 = jnp.bfloat16
_F32 = jnp.float32


def _dot(a, b):
    return jnp.dot(a, b, preferred_element_type=_F32)


def _dot_nt(a, b):
    return lax.dot_general(a, b, (((1,), (1,)), ((), ())), preferred_element_type=_F32)


def _split_bf16(x):
    hi = x.astype(_BF16)
    lo = (x - hi.astype(_F32)).astype(_BF16)
    return hi, lo


def _alibi_slopes():
    return np.exp2(-ALIBI_MAX * (np.arange(N_HEADS) + 1.0) / N_HEADS)


def _resident(shape):
    zeros = (0,) * len(shape)
    return pl.BlockSpec(shape, lambda *_: zeros, pipeline_mode=pl.Buffered(1))


def _ffn_tail(x, g_ref, w1_ref, w3_ref, w2_ref, o_ref):
    ms = jnp.mean(x * x, axis=-1, keepdims=True)
    h = (x * lax.rsqrt(ms + EPS) * g_ref[...]).astype(_BF16)
    acc = jnp.zeros(x.shape, _F32)
    for c in range(D_FF // FF_CHUNK):
        sl = slice(c * FF_CHUNK, (c + 1) * FF_CHUNK)
        a = _dot(h, w1_ref[:, sl])
        b = _dot(h, w3_ref[:, sl])
        gl = (a * jax.nn.sigmoid(a) * b).astype(_BF16)
        acc = acc + _dot(gl, w2_ref[sl, :])
    o_ref[...] = x + 0.5 * acc


def _ffn_kernel(x_ref, g_ref, w1_ref, w3_ref, w2_ref, o_ref):
    _ffn_tail(x_ref[...], g_ref, w1_ref, w3_ref, w2_ref, o_ref)


def _mix_ffn_kernel(x_ref, a_ref, c_ref, wo_ref, g_ref, w1_ref, w3_ref, w2_ref, o_ref):
    x = (x_ref[...] + _dot(a_ref[...], wo_ref[:ATTN_WIDTH, :])
         + _dot(c_ref[...], wo_ref[ATTN_WIDTH:, :]))
    _ffn_tail(x, g_ref, w1_ref, w3_ref, w2_ref, o_ref)


def _ffn(x2d, g, w1, w3, w2, mix=None):
    m = x2d.shape[0]
    rows = pl.BlockSpec((FFN_ROWS, D_MODEL), lambda i: (i, 0))
    w_specs = [_resident((1, D_MODEL)), _resident((D_MODEL, D_FF)),
               _resident((D_MODEL, D_FF)), _resident((D_FF, D_MODEL))]
    if mix is None:
        body, ins, specs = _ffn_kernel, (x2d,), [rows]
    else:
        attn, conv, wo = mix
        half = pl.BlockSpec((FFN_ROWS, ATTN_WIDTH), lambda i: (i, 0))
        body, ins = _mix_ffn_kernel, (x2d, attn, conv, wo)
        specs = [rows, half, half, _resident((D_MODEL, D_MODEL))]
    return pl.pallas_call(
        body,
        grid=(m // FFN_ROWS,),
        in_specs=specs + w_specs,
        out_specs=rows,
        out_shape=jax.ShapeDtypeStruct((m, D_MODEL), _F32),
        compiler_params=pltpu.CompilerParams(
            dimension_semantics=("arbitrary",), vmem_limit_bytes=VMEM_LIMIT),
        name="ffn_mix" if mix is not None else "ffn",
    )(*ins, g, w1, w3, w2)


def _head_rms(z, e_ref, gain):
    ms = _dot((z * z).astype(_BF16), e_ref[...])
    return z * lax.rsqrt(ms + EPS) * gain


def _in_proj_kernel(x_ref, g_ref, w_ref, e_ref, gq_ref, gk_ref, hmask_ref, slope_ref,
                    q_ref, k_ref, v_ref, u_ref, f_ref, km_ref):
    t = pl.program_id(1)
    nb = km_ref.shape[0]

    @pl.when(t == 0)
    def _():
        km_ref[...] = jnp.zeros(km_ref.shape, _F32)

    x = x_ref[0]
    ms = jnp.mean(x * x, axis=-1, keepdims=True)
    h = (x * lax.rsqrt(ms + EPS) * g_ref[...]).astype(_BF16)
    a = ATTN_WIDTH
    qn = _head_rms(_dot(h, w_ref[:, 0:a]), e_ref, gq_ref[...])
    kn = _head_rms(_dot(h, w_ref[:, a:2 * a]), e_ref, gk_ref[...])
    q_ref[0] = (qn * (HEAD_DIM ** -0.5)).astype(_BF16)
    k_ref[0] = kn.astype(_BF16)

    n_idx = lax.broadcasted_iota(jnp.int32, (N_HEADS, nb, BLOCK), 1)
    slope_blk = slope_ref[...].reshape(N_HEADS, nb, 1)
    extra_rows = lax.broadcasted_iota(jnp.int32, (LANES - 2 * nb, BLOCK), 0)
    q_off = lax.broadcasted_iota(jnp.int32, (LANES - 2 * nb, BLOCK), 1).astype(_F32)
    slopes = _alibi_slopes()
    extras = [jnp.where(extra_rows < 2, 1.0,
                        jnp.where(extra_rows == 2, -float(slopes[2 * p]) * q_off,
                                  jnp.where(extra_rows == 3, -float(slopes[2 * p + 1]) * q_off, 0.0)))
              for p in range(N_PAIRS)]
    for sb in range(PROJ_ROWS // BLOCK):
        i = t * (PROJ_ROWS // BLOCK) + sb
        rows = slice(sb * BLOCK, (sb + 1) * BLOCK)
        km = km_ref[...]
        kmt = jnp.concatenate([km] * N_HEADS, axis=0) * hmask_ref[...]
        km_hi, km_lo = _split_bf16(kmt)
        q_hi, q_lo = _split_bf16(qn[rows])
        gt = _dot_nt(km_hi, q_hi) + _dot_nt(km_hi, q_lo) + _dot_nt(km_lo, q_hi)
        g3 = gt.reshape(N_HEADS, nb, BLOCK)
        past = n_idx < i
        left = jnp.where(past, g3, -jnp.inf)
        sel = jnp.zeros(g3.shape, jnp.bool_)
        for _ in range(TOP_K):
            best = jnp.max(left, axis=1, keepdims=True)
            first = jnp.min(jnp.where(left == best, n_idx, nb), axis=1, keepdims=True)
            pick = n_idx == first
            sel = sel | pick
            left = jnp.where(pick, -jnp.inf, left)
        sel = sel & past
        alibi_blk = slope_blk * (n_idx - i).astype(_F32)
        bias = jnp.where(sel, alibi_blk, jnp.where(n_idx == i, 0.0, MASK_BIAS))
        bias2 = bias.reshape(N_HEADS * nb, BLOCK)
        for p in range(N_PAIRS):
            ft = jnp.concatenate([bias2[2 * nb * p:2 * nb * (p + 1)], extras[p]], axis=0)
            f_ref[0, p, rows, :] = ft.T.astype(_BF16)
        km_ref[pl.ds(i, 1), :] = jnp.mean(kn[rows], axis=0, keepdims=True)
    v_ref[0] = _dot(h, w_ref[:, 2 * a:3 * a]).astype(_BF16)
    za = _dot(h, w_ref[:, 3 * a:3 * a + CONV_WIDTH])
    zg = _dot(h, w_ref[:, 3 * a + CONV_WIDTH:])
    u_ref[0] = (za * jax.nn.sigmoid(zg)).astype(_BF16)


def _in_proj(x1, g, w_in, gq, gk):
    b, s, _ = x1.shape
    nb = s // BLOCK
    assert 2 * nb + 4 <= LANES
    head_of_col = np.arange(ATTN_WIDTH) // HEAD_DIM
    e = jnp.asarray((head_of_col[:, None] == head_of_col[None, :]) / HEAD_DIM, _BF16)
    hmask = jnp.asarray((np.arange(N_HEADS * nb) // nb)[:, None] == head_of_col[None, :], _F32)
    slope_blk = jnp.asarray(np.repeat(_alibi_slopes() * BLOCK, nb)[:, None], _F32)
    tile = lambda w: pl.BlockSpec((1, PROJ_ROWS, w), lambda bi, ti: (bi, ti, 0))
    act = jax.ShapeDtypeStruct((b, s, ATTN_WIDTH), _BF16)
    return pl.pallas_call(
        _in_proj_kernel,
        grid=(b, s // PROJ_ROWS),
        in_specs=[tile(D_MODEL), _resident((1, D_MODEL)), _resident((D_MODEL, IN_COLS)),
                  _resident((ATTN_WIDTH, ATTN_WIDTH)), _resident((1, ATTN_WIDTH)),
                  _resident((1, ATTN_WIDTH)), _resident((N_HEADS * nb, ATTN_WIDTH)),
                  _resident((N_HEADS * nb, 1))],
        out_specs=[tile(ATTN_WIDTH)] * 4
        + [pl.BlockSpec((1, N_PAIRS, PROJ_ROWS, LANES), lambda bi, ti: (bi, 0, ti, 0))],
        out_shape=[act] * 4 + [jax.ShapeDtypeStruct((b, N_PAIRS, s, LANES), _BF16)],
        scratch_shapes=[pltpu.VMEM((nb, ATTN_WIDTH), _F32)],
        compiler_params=pltpu.CompilerParams(
            dimension_semantics=("arbitrary", "arbitrary"), vmem_limit_bytes=VMEM_LIMIT),
        name="in_proj",
    )(x1, g, w_in, e, gq, gk, hmask, slope_blk)


CONV_RC = 64


def _conv_kernel(u_ref, w_ref, b_ref, lg_ref, lb_ref, o_ref, ubuf_ref, y_ref):
    t = pl.program_id(1)
    n_chunks = CONV_WIDTH // LANES

    @pl.when(t == 0)
    def _():
        ubuf_ref[:, 0:CONV_HALO, :] = jnp.zeros((n_chunks, CONV_HALO, LANES), _F32)

    @pl.when(t > 0)
    def _():
        ubuf_ref[:, 0:CONV_HALO, :] = ubuf_ref[:, CONV_ROWS:CONV_ROWS + CONV_HALO, :]

    for lc in range(n_chunks):
        ln = slice(lc * LANES, (lc + 1) * LANES)
        ubuf_ref[lc, CONV_HALO:, :] = u_ref[0, :, ln].astype(_F32)
        for rc in range(CONV_ROWS // CONV_RC):
            r0 = rc * CONV_RC
            views = {}

            def view(phase, back):
                if (phase, back) not in views:
                    start = r0 + CONV_HALO - 8 * back + phase
                    views[phase, back] = ubuf_ref[lc, pl.ds(start, CONV_RC // 8, stride=8), :]
                return views[phase, back]

            for j in range(8):
                acc = jnp.broadcast_to(b_ref[:, ln], (CONV_RC // 8, LANES))
                for k in range(CONV_K):
                    d = CONV_K - 1 - k
                    acc = acc + w_ref[k:k + 1, ln] * view((j - d) % 8, -((j - d) // 8))
                y_ref[lc, pl.ds(r0 + j, CONV_RC // 8, stride=8), :] = acc
    y = jnp.concatenate([y_ref[lc] for lc in range(n_chunks)], axis=1)
    mu = jnp.mean(y, axis=-1, keepdims=True)
    d = y - mu
    var = jnp.mean(d * d, axis=-1, keepdims=True)
    z = d * lax.rsqrt(var + EPS) * lg_ref[...] + lb_ref[...]
    o_ref[0] = (z * jax.nn.sigmoid(z)).astype(_BF16)


def _conv(u, w, bias, ln_g, ln_b):
    b, s, c = u.shape
    kpad = 8 * (-(-CONV_K // 8))
    tile = pl.BlockSpec((1, CONV_ROWS, c), lambda bi, ti: (bi, ti, 0))
    return pl.pallas_call(
        _conv_kernel,
        grid=(b, s // CONV_ROWS),
        in_specs=[tile, _resident((kpad, c)), _resident((1, c)), _resident((1, c)),
                  _resident((1, c))],
        out_specs=tile,
        out_shape=jax.ShapeDtypeStruct((b, s, c), _BF16),
        scratch_shapes=[pltpu.VMEM((c // LANES, CONV_ROWS + CONV_HALO, LANES), _F32),
                        pltpu.VMEM((c // LANES, CONV_ROWS, LANES), _F32)],
        compiler_params=pltpu.CompilerParams(
            dimension_semantics=("arbitrary", "arbitrary"), vmem_limit_bytes=VMEM_LIMIT),
        name="conv",
    )(u, jnp.pad(w, ((0, kpad - CONV_K), (0, 0))), bias, ln_g, ln_b)


def _stacked_queries(q, f, nb):
    lane = lax.broadcasted_iota(jnp.int32, (BLOCK, LANES), 1)
    zero = jnp.zeros((BLOCK, LANES), _BF16)
    halves = []
    for sl in range(2):
        head_lanes = (lane >= sl * HEAD_DIM) & (lane < (sl + 1) * HEAD_DIM)
        feat_lanes = (((lane >= sl * nb) & (lane < (sl + 1) * nb)) | (lane == 2 * nb + sl)
                      | (lane == 2 * nb + 2 + sl))
        halves.append(jnp.concatenate([jnp.where(head_lanes, q, zero),
                                       jnp.where(feat_lanes, f, zero)], axis=1))
    return jnp.concatenate(halves, axis=0)


def _causal_mask():
    return (lax.broadcasted_iota(jnp.int32, (2 * BLOCK, BLOCK), 1)
            <= lax.broadcasted_iota(jnp.int32, (2 * BLOCK, BLOCK), 0) % BLOCK)


def _merge_heads(acc):
    lane = lax.broadcasted_iota(jnp.int32, (BLOCK, LANES), 1)
    out = acc[:, :LANES] / acc[:, LANES:]
    return jnp.where(lane < HEAD_DIM, out[:BLOCK], out[BLOCK:]).astype(_BF16)


def _attn_row_block(i, nb, shift, q_ref, f_ref, o_ref, kaug_ref, vaugt_ref):
    rows = slice(i * BLOCK, (i + 1) * BLOCK)
    n = (i + 1) * BLOCK
    qa = _stacked_queries(q_ref[0, rows, :], f_ref[0, 0, rows, :], nb)
    qat = qa.astype(_F32).T.astype(_BF16)
    st = _dot(kaug_ref[0:n, :], qat)
    key = lax.broadcasted_iota(jnp.int32, (BLOCK, 2 * BLOCK), 0)
    qry = lax.broadcasted_iota(jnp.int32, (BLOCK, 2 * BLOCK), 1) % BLOCK
    probs = []
    for j in range(i + 1):
        s = st[j * BLOCK:(j + 1) * BLOCK]
        if j == i:
            s = jnp.where(key <= qry, s, NEG)
        probs.append(jnp.exp(s - shift).astype(_BF16))
    acc = _dot(vaugt_ref[:, 0:n], jnp.concatenate(probs, axis=0))
    sums = acc[LANES:LANES + 1, :]
    out = acc[:LANES, :] / sums
    merged = jnp.concatenate([out[:HEAD_DIM, :BLOCK], out[HEAD_DIM:, BLOCK:]], axis=0)
    o_ref[0, rows, :] = merged.T.astype(_BF16)
    return sums


def _attn_row_block_running_max(i, nb, q_ref, f_ref, o_ref, kaug_ref, vaug_ref, acc_ref, m_ref):
    rows = pl.ds(pl.multiple_of(i * BLOCK, BLOCK), BLOCK)
    qa = _stacked_queries(q_ref[0, rows, :], f_ref[0, 0, rows, :], nb)
    m_ref[...] = jnp.full(m_ref.shape, NEG, _F32)
    acc_ref[...] = jnp.zeros(acc_ref.shape, _F32)

    def key_block(j, carry):
        keys = pl.ds(pl.multiple_of(j * BLOCK, BLOCK), BLOCK)
        s = _dot_nt(qa, kaug_ref[keys, :])
        s = jnp.where(_causal_mask() | (j < i), s, NEG)
        m_old = m_ref[...]
        m_new = jnp.maximum(m_old, jnp.max(s, axis=1, keepdims=True))
        m_ref[...] = m_new
        p = jnp.exp(s - jnp.concatenate([m_new, m_new], axis=1)).astype(_BF16)
        rescale = jnp.exp(m_old - m_new)
        acc_ref[...] = (acc_ref[...] * jnp.concatenate([rescale, rescale], axis=1)
                        + _dot(p, vaug_ref[keys, :]))
        return carry

    lax.fori_loop(0, i + 1, key_block, 0)
    o_ref[0, rows, :] = _merge_heads(acc_ref[...])


def _attn_kernel(shift_ref, q_ref, f_ref, k_ref, v_ref, kf_ref, o_ref,
                 kaug_ref, vaugt_ref, vaug_ref, acc_ref, m_ref):
    nb = kaug_ref.shape[0] // BLOCK
    kaug_ref[:, :LANES] = k_ref[0]
    kaug_ref[:, LANES:] = kf_ref[0]
    vaugt_ref[:LANES, :] = v_ref[0].astype(_F32).T.astype(_BF16)
    vaugt_ref[LANES:, :] = jnp.ones((SUM_ROWS, vaugt_ref.shape[1]), _BF16)
    shift = shift_ref[0, 0]
    min_sum = None
    for i in range(nb):
        sums = _attn_row_block(i, nb, shift, q_ref, f_ref, o_ref, kaug_ref, vaugt_ref)
        min_sum = sums if min_sum is None else jnp.minimum(min_sum, sums)
    healthy = jnp.min(min_sum) >= MIN_ROW_SUM

    @pl.when(jnp.logical_not(healthy))
    def _():
        vaug_ref[:, :LANES] = v_ref[0]
        vaug_ref[:, LANES:] = jnp.ones((vaug_ref.shape[0], LANES), _BF16)

        def row_block(i, carry):
            _attn_row_block_running_max(i, nb, q_ref, f_ref, o_ref, kaug_ref, vaug_ref,
                                        acc_ref, m_ref)
            return carry

        lax.fori_loop(0, nb, row_block, 0)


def _key_features(s):
    nb = s // BLOCK
    slopes = _alibi_slopes()
    pos = np.arange(s)
    feat = np.zeros((N_PAIRS, s, LANES), np.float32)
    onehot = (pos[:, None] // BLOCK == np.arange(nb)[None, :]).astype(np.float32)
    feat[:, :, 0:nb] = onehot
    feat[:, :, nb:2 * nb] = onehot
    for p in range(N_PAIRS):
        for slot in range(2):
            feat[p, :, 2 * nb + slot] = slopes[2 * p + slot] * (pos % BLOCK)
            feat[p, :, 2 * nb + 2 + slot] = 1.0
    assert np.array_equal(feat.astype(_BF16).astype(np.float32), feat)
    return jnp.asarray(feat, _BF16)


def _attention(q, f, k, v, shift):
    b, s, _ = q.shape
    kf = _key_features(s)
    seq = pl.BlockSpec((1, s, LANES), lambda bi, p: (bi, 0, p))
    return pl.pallas_call(
        _attn_kernel,
        grid=(b, N_PAIRS),
        in_specs=[pl.BlockSpec(memory_space=pltpu.SMEM), seq,
                  pl.BlockSpec((1, 1, s, LANES), lambda bi, p: (bi, p, 0, 0)),
                  seq, seq,
                  pl.BlockSpec((1, s, LANES), lambda bi, p: (p, 0, 0))],
        out_specs=seq,
        out_shape=jax.ShapeDtypeStruct((b, s, ATTN_WIDTH), _BF16),
        scratch_shapes=[pltpu.VMEM((s, 2 * LANES), _BF16),
                        pltpu.VMEM((LANES + SUM_ROWS, s), _BF16),
                        pltpu.VMEM((s, 2 * LANES), _BF16),
                        pltpu.VMEM((2 * BLOCK, 2 * LANES), _F32),
                        pltpu.VMEM((2 * BLOCK, LANES), _F32)],
        compiler_params=pltpu.CompilerParams(
            dimension_semantics=("arbitrary", "arbitrary"), vmem_limit_bytes=VMEM_LIMIT),
        name="moba_attention",
    )(shift, q, f, k, v, kf)


def kernel(x, ffn1_norm, ffn1_w1, ffn1_w3, ffn1_w2, mix_norm, w_in, q_norm, k_norm,
           conv_dw_w, conv_dw_b, conv_ln_g, conv_ln_b, w_out, ffn2_norm, ffn2_w1,
           ffn2_w3, ffn2_w2):
    b, s, d = x.shape
    assert d == D_MODEL and s % PROJ_ROWS == 0 and (b * s) % FFN_ROWS == 0
    bf = lambda w: w.astype(_BF16)
    row = lambda p: p.reshape(1, -1).astype(_F32)
    for l in range(ffn1_norm.shape[0]):
        x1 = _ffn(x.reshape(b * s, d), row(ffn1_norm[l]), bf(ffn1_w1[l]), bf(ffn1_w3[l]),
                  bf(ffn1_w2[l]))
        q, k, v, u, f = _in_proj(x1.reshape(b, s, d), row(mix_norm[l]), bf(w_in[l]),
                                 row(jnp.tile(q_norm[l], N_HEADS)),
                                 row(jnp.tile(k_norm[l], N_HEADS)))
        conv = _conv(u, conv_dw_w[l].astype(_F32), row(conv_dw_b[l]), row(conv_ln_g[l]),
                     row(conv_ln_b[l]))
        shift = (HEAD_DIM ** 0.5 * jnp.max(jnp.abs(q_norm[l])) * jnp.max(jnp.abs(k_norm[l]))
                 ).astype(_F32).reshape(1, 1)
        attn = _attention(q, f, k, v, shift)
        x = _ffn(x1, row(ffn2_norm[l]), bf(ffn2_w1[l]), bf(ffn2_w3[l]), bf(ffn2_w2[l]),
                 mix=(attn.reshape(b * s, ATTN_WIDTH), conv.reshape(b * s, CONV_WIDTH),
                      bf(w_out[l]))).reshape(b, s, d)
    return x
```

```python
import functools

import numpy as np
import jax
import jax.numpy as jnp
from jax import lax
from jax.experimental import pallas as pl
from jax.experimental.pallas import tpu as pltpu

D_MODEL = 1024
ATTN_WIDTH = 512
CONV_WIDTH = 512
HEAD_DIM = 64
N_HEADS = 8
N_PAIRS = N_HEADS // 2
IN_COLS = 3 * ATTN_WIDTH + 2 * CONV_WIDTH
CONV_K = 31
BLOCK = 256
TOP_K = 3
D_FF = 2816
ALIBI_MAX = 8.0
EPS = 1e-6
MASK_BIAS = -float(2 ** 30)
NEG = -1e30
MIN_ROW_SUM = 2.0 ** -58

LANES = 128
SUM_ROWS = 16
FF_CHUNK = 256
FFN_ROWS = 512
PROJ_ROWS = 1024
CONV_ROWS = 512
CONV_HALO = 32
VMEM_LIMIT = 56 * 1024 * 1024

_BF16 = jnp.bfloat16
_F32 = jnp.float32


def _dot(a, b):
    return jnp.dot(a, b, preferred_element_type=_F32)


def _dot_nt(a, b):
    return lax.dot_general(a, b, (((1,), (1,)), ((), ())), preferred_element_type=_F32)


def _split_bf16(x):
    hi = x.astype(_BF16)
    lo = (x - hi.astype(_F32)).astype(_BF16)
    return hi, lo


def _alibi_slopes():
    return np.exp2(-ALIBI_MAX * (np.arange(N_HEADS) + 1.0) / N_HEADS)


def _resident(shape):
    zeros = (0,) * len(shape)
    return pl.BlockSpec(shape, lambda *_: zeros, pipeline_mode=pl.Buffered(1))


def _ffn_tail(x, g_ref, w1_ref, w3_ref, w2_ref, o_ref):
    ms = jnp.mean(x * x, axis=-1, keepdims=True)
    h = (x * lax.rsqrt(ms + EPS) * g_ref[...]).astype(_BF16)
    acc = jnp.zeros(x.shape, _F32)
    for c in range(D_FF // FF_CHUNK):
        sl = slice(c * FF_CHUNK, (c + 1) * FF_CHUNK)
        a = _dot(h, w1_ref[:, sl])
        b = _dot(h, w3_ref[:, sl])
        gl = (a * jax.nn.sigmoid(a) * b).astype(_BF16)
        acc = acc + _dot(gl, w2_ref[sl, :])
    o_ref[...] = x + 0.5 * acc


def _ffn_kernel(x_ref, g_ref, w1_ref, w3_ref, w2_ref, o_ref):
    _ffn_tail(x_ref[...], g_ref, w1_ref, w3_ref, w2_ref, o_ref)


def _mix_ffn_kernel(x_ref, a_ref, c_ref, wo_ref, g_ref, w1_ref, w3_ref, w2_ref, o_ref):
    x = (x_ref[...] + _dot(a_ref[...], wo_ref[:ATTN_WIDTH, :])
         + _dot(c_ref[...], wo_ref[ATTN_WIDTH:, :]))
    _ffn_tail(x, g_ref, w1_ref, w3_ref, w2_ref, o_ref)


def _ffn(x2d, g, w1, w3, w2, mix=None):
    m = x2d.shape[0]
    rows = pl.BlockSpec((FFN_ROWS, D_MODEL), lambda i: (i, 0))
    w_specs = [_resident((1, D_MODEL)), _resident((D_MODEL, D_FF)),
               _resident((D_MODEL, D_FF)), _resident((D_FF, D_MODEL))]
    if mix is None:
        body, ins, specs = _ffn_kernel, (x2d,), [rows]
    else:
        attn, conv, wo = mix
        half = pl.BlockSpec((FFN_ROWS, ATTN_WIDTH), lambda i: (i, 0))
        body, ins = _mix_ffn_kernel, (x2d, attn, conv, wo)
        specs = [rows, half, half, _resident((D_MODEL, D_MODEL))]
    return pl.pallas_call(
        body,
        grid=(m // FFN_ROWS,),
        in_specs=specs + w_specs,
        out_specs=rows,
        out_shape=jax.ShapeDtypeStruct((m, D_MODEL), _F32),
        compiler_params=pltpu.CompilerParams(
            dimension_semantics=("arbitrary",), vmem_limit_bytes=VMEM_LIMIT),
        name="ffn_mix" if mix is not None else "ffn",
    )(*ins, g, w1, w3, w2)


def _head_rms(z, e_ref, gain):
    ms = _dot((z * z).astype(_BF16), e_ref[...])
    return z * lax.rsqrt(ms + EPS) * gain


def _in_proj_kernel(x_ref, g_ref, w_ref, e_ref, gq_ref, gk_ref, hmask_ref, slope_ref,
                    q_ref, k_ref, v_ref, u_ref, f_ref, km_ref):
    t = pl.program_id(1)
    nb = km_ref.shape[0]

    @pl.when(t == 0)
    def _():
        km_ref[...] = jnp.zeros(km_ref.shape, _F32)

    x = x_ref[0]
    ms = jnp.mean(x * x, axis=-1, keepdims=True)
    h = (x * lax.rsqrt(ms + EPS) * g_ref[...]).astype(_BF16)
    a = ATTN_WIDTH
    qn = _head_rms(_dot(h, w_ref[:, 0:a]), e_ref, gq_ref[...])
    kn = _head_rms(_dot(h, w_ref[:, a:2 * a]), e_ref, gk_ref[...])
    q_ref[0] = (qn * (HEAD_DIM ** -0.5)).astype(_BF16)
    k_ref[0] = kn.astype(_BF16)

    n_idx = lax.broadcasted_iota(jnp.int32, (N_HEADS, nb, BLOCK), 1)
    slope_blk = slope_ref[...].reshape(N_HEADS, nb, 1)
    extra_rows = lax.broadcasted_iota(jnp.int32, (LANES - 2 * nb, BLOCK), 0)
    q_off = lax.broadcasted_iota(jnp.int32, (LANES - 2 * nb, BLOCK), 1).astype(_F32)
    slopes = _alibi_slopes()
    extras = [jnp.where(extra_rows < 2, 1.0,
                        jnp.where(extra_rows == 2, -float(slopes[2 * p]) * q_off,
                                  jnp.where(extra_rows == 3, -float(slopes[2 * p + 1]) * q_off, 0.0)))
              for p in range(N_PAIRS)]
    for sb in range(PROJ_ROWS // BLOCK):
        i = t * (PROJ_ROWS // BLOCK) + sb
        rows = slice(sb * BLOCK, (sb + 1) * BLOCK)
        km = km_ref[...]
        kmt = jnp.concatenate([km] * N_HEADS, axis=0) * hmask_ref[...]
        km_hi, km_lo = _split_bf16(kmt)
        q_hi, q_lo = _split_bf16(qn[rows])
        gt = _dot_nt(km_hi, q_hi) + _dot_nt(km_hi, q_lo) + _dot_nt(km_lo, q_hi)
        g3 = gt.reshape(N_HEADS, nb, BLOCK)
        past = n_idx < i
        left = jnp.where(past, g3, -jnp.inf)
        sel = jnp.zeros(g3.shape, jnp.bool_)
        for _ in range(TOP_K):
            best = jnp.max(left, axis=1, keepdims=True)
            first = jnp.min(jnp.where(left == best, n_idx, nb), axis=1, keepdims=True)
            pick = n_idx == first
            sel = sel | pick
            left = jnp.where(pick, -jnp.inf, left)
        sel = sel & past
        alibi_blk = slope_blk * (n_idx - i).astype(_F32)
        bias = jnp.where(sel, alibi_blk, jnp.where(n_idx == i, 0.0, MASK_BIAS))
        bias2 = bias.reshape(N_HEADS * nb, BLOCK)
        for p in range(N_PAIRS):
            ft = jnp.concatenate([bias2[2 * nb * p:2 * nb * (p + 1)], extras[p]], axis=0)
            f_ref[0, p, rows, :] = ft.T.astype(_BF16)
        km_ref[pl.ds(i, 1), :] = jnp.mean(kn[rows], axis=0, keepdims=True)
    v_ref[0] = _dot(h, w_ref[:, 2 * a:3 * a]).astype(_BF16)
    za = _dot(h, w_ref[:, 3 * a:3 * a + CONV_WIDTH])
    zg = _dot(h, w_ref[:, 3 * a + CONV_WIDTH:])
    u_ref[0] = (za * jax.nn.sigmoid(zg)).astype(_BF16)


def _in_proj(x1, g, w_in, gq, gk):
    b, s, _ = x1.shape
    nb = s // BLOCK
    assert 2 * nb + 4 <= LANES
    head_of_col = np.arange(ATTN_WIDTH) // HEAD_DIM
    e = jnp.asarray((head_of_col[:, None] == head_of_col[None, :]) / HEAD_DIM, _BF16)
    hmask = jnp.asarray((np.arange(N_HEADS * nb) // nb)[:, None] == head_of_col[None, :], _F32)
    slope_blk = jnp.asarray(np.repeat(_alibi_slopes() * BLOCK, nb)[:, None], _F32)
    tile = lambda w: pl.BlockSpec((1, PROJ_ROWS, w), lambda bi, ti: (bi, ti, 0))
    act = jax.ShapeDtypeStruct((b, s, ATTN_WIDTH), _BF16)
    return pl.pallas_call(
        _in_proj_kernel,
        grid=(b, s // PROJ_ROWS),
        in_specs=[tile(D_MODEL), _resident((1, D_MODEL)), _resident((D_MODEL, IN_COLS)),
                  _resident((ATTN_WIDTH, ATTN_WIDTH)), _resident((1, ATTN_WIDTH)),
                  _resident((1, ATTN_WIDTH)), _resident((N_HEADS * nb, ATTN_WIDTH)),
                  _resident((N_HEADS * nb, 1))],
        out_specs=[tile(ATTN_WIDTH)] * 4
        + [pl.BlockSpec((1, N_PAIRS, PROJ_ROWS, LANES), lambda bi, ti: (bi, 0, ti, 0))],
        out_shape=[act] * 4 + [jax.ShapeDtypeStruct((b, N_PAIRS, s, LANES), _BF16)],
        scratch_shapes=[pltpu.VMEM((nb, ATTN_WIDTH), _F32)],
        compiler_params=pltpu.CompilerParams(
            dimension_semantics=("arbitrary", "arbitrary"), vmem_limit_bytes=VMEM_LIMIT),
        name="in_proj",
    )(x1, g, w_in, e, gq, gk, hmask, slope_blk)


CONV_RC = 64


def _conv_kernel(u_ref, w_ref, b_ref, lg_ref, lb_ref, o_ref, ubuf_ref, y_ref):
    t = pl.program_id(1)
    n_chunks = CONV_WIDTH // LANES

    @pl.when(t == 0)
    def _():
        ubuf_ref[:, 0:CONV_HALO, :] = jnp.zeros((n_chunks, CONV_HALO, LANES), _F32)

    @pl.when(t > 0)
    def _():
        ubuf_ref[:, 0:CONV_HALO, :] = ubuf_ref[:, CONV_ROWS:CONV_ROWS + CONV_HALO, :]

    for lc in range(n_chunks):
        ln = slice(lc * LANES, (lc + 1) * LANES)
        ubuf_ref[lc, CONV_HALO:, :] = u_ref[0, :, ln].astype(_F32)
        for rc in range(CONV_ROWS // CONV_RC):
            r0 = rc * CONV_RC
            views = {}

            def view(phase, back):
                if (phase, back) not in views:
                    start = r0 + CONV_HALO - 8 * back + phase
                    views[phase, back] = ubuf_ref[lc, pl.ds(start, CONV_RC // 8, stride=8), :]
                return views[phase, back]

            for j in range(8):
                acc = jnp.broadcast_to(b_ref[:, ln], (CONV_RC // 8, LANES))
                for k in range(CONV_K):
                    d = CONV_K - 1 - k
                    acc = acc + w_ref[k:k + 1, ln] * view((j - d) % 8, -((j - d) // 8))
                y_ref[lc, pl.ds(r0 + j, CONV_RC // 8, stride=8), :] = acc
    y = jnp.concatenate([y_ref[lc] for lc in range(n_chunks)], axis=1)
    mu = jnp.mean(y, axis=-1, keepdims=True)
    d = y - mu
    var = jnp.mean(d * d, axis=-1, keepdims=True)
    z = d * lax.rsqrt(var + EPS) * lg_ref[...] + lb_ref[...]
    o_ref[0] = (z * jax.nn.sigmoid(z)).astype(_BF16)


def _conv(u, w, bias, ln_g, ln_b):
    b, s, c = u.shape
    kpad = 8 * (-(-CONV_K // 8))
    tile = pl.BlockSpec((1, CONV_ROWS, c), lambda bi, ti: (bi, ti, 0))
    return pl.pallas_call(
        _conv_kernel,
        grid=(b, s // CONV_ROWS),
        in_specs=[tile, _resident((kpad, c)), _resident((1, c)), _resident((1, c)),
                  _resident((1, c))],
        out_specs=tile,
        out_shape=jax.ShapeDtypeStruct((b, s, c), _BF16),
        scratch_shapes=[pltpu.VMEM((c // LANES, CONV_ROWS + CONV_HALO, LANES), _F32),
                        pltpu.VMEM((c // LANES, CONV_ROWS, LANES), _F32)],
        compiler_params=pltpu.CompilerParams(
            dimension_semantics=("arbitrary", "arbitrary"), vmem_limit_bytes=VMEM_LIMIT),
        name="conv",
    )(u, jnp.pad(w, ((0, kpad - CONV_K), (0, 0))), bias, ln_g, ln_b)


def _stacked_queries(q, f, nb):
    lane = lax.broadcasted_iota(jnp.int32, (BLOCK, LANES), 1)
    zero = jnp.zeros((BLOCK, LANES), _BF16)
    halves = []
    for sl in range(2):
        head_lanes = (lane >= sl * HEAD_DIM) & (lane < (sl + 1) * HEAD_DIM)
        feat_lanes = (((lane >= sl * nb) & (lane < (sl + 1) * nb)) | (lane == 2 * nb + sl)
                      | (lane == 2 * nb + 2 + sl))
        halves.append(jnp.concatenate([jnp.where(head_lanes, q, zero),
                                       jnp.where(feat_lanes, f, zero)], axis=1))
    return jnp.concatenate(halves, axis=0)


def _causal_mask():
    return (lax.broadcasted_iota(jnp.int32, (2 * BLOCK, BLOCK), 1)
            <= lax.broadcasted_iota(jnp.int32, (2 * BLOCK, BLOCK), 0) % BLOCK)


def _merge_heads(acc):
    lane = lax.broadcasted_iota(jnp.int32, (BLOCK, LANES), 1)
    out = acc[:, :LANES] / acc[:, LANES:]
    return jnp.where(lane < HEAD_DIM, out[:BLOCK], out[BLOCK:]).astype(_BF16)


def _attn_row_block(i, nb, shift, q_ref, f_ref, o_ref, kaug_ref, vaugt_ref):
    rows = slice(i * BLOCK, (i + 1) * BLOCK)
    n = (i + 1) * BLOCK
    qa = _stacked_queries(q_ref[0, rows, :], f_ref[0, 0, rows, :], nb)
    qat = qa.astype(_F32).T.astype(_BF16)
    st = _dot(kaug_ref[0:n, :], qat)
    key = lax.broadcasted_iota(jnp.int32, (BLOCK, 2 * BLOCK), 0)
    qry = lax.broadcasted_iota(jnp.int32, (BLOCK, 2 * BLOCK), 1) % BLOCK
    probs = []
    for j in range(i + 1):
        s = st[j * BLOCK:(j + 1) * BLOCK]
        if j == i:
            s = jnp.where(key <= qry, s, NEG)
        probs.append(jnp.exp(s - shift).astype(_BF16))
    acc = _dot(vaugt_ref[:, 0:n], jnp.concatenate(probs, axis=0))
    sums = acc[LANES:LANES + 1, :]
    out = acc[:LANES, :] / sums
    merged = jnp.concatenate([out[:HEAD_DIM, :BLOCK], out[HEAD_DIM:, BLOCK:]], axis=0)
    o_ref[0, rows, :] = merged.T.astype(_BF16)
    return sums


def _attn_row_block_running_max(i, nb, q_ref, f_ref, o_ref, kaug_ref, vaug_ref, acc_ref, m_ref):
    rows = pl.ds(pl.multiple_of(i * BLOCK, BLOCK), BLOCK)
    qa = _stacked_queries(q_ref[0, rows, :], f_ref[0, 0, rows, :], nb)
    m_ref[...] = jnp.full(m_ref.shape, NEG, _F32)
    acc_ref[...] = jnp.zeros(acc_ref.shape, _F32)

    def key_block(j, carry):
        keys = pl.ds(pl.multiple_of(j * BLOCK, BLOCK), BLOCK)
        s = _dot_nt(qa, kaug_ref[keys, :])
        s = jnp.where(_causal_mask() | (j < i), s, NEG)
        m_old = m_ref[...]
        m_new = jnp.maximum(m_old, jnp.max(s, axis=1, keepdims=True))
        m_ref[...] = m_new
        p = jnp.exp(s - jnp.concatenate([m_new, m_new], axis=1)).astype(_BF16)
        rescale = jnp.exp(m_old - m_new)
        acc_ref[...] = (acc_ref[...] * jnp.concatenate([rescale, rescale], axis=1)
                        + _dot(p, vaug_ref[keys, :]))
        return carry

    lax.fori_loop(0, i + 1, key_block, 0)
    o_ref[0, rows, :] = _merge_heads(acc_ref[...])


def _attn_kernel(shift_ref, q_ref, f_ref, k_ref, v_ref, kf_ref, o_ref,
                 kaug_ref, vaugt_ref, vaug_ref, acc_ref, m_ref):
    nb = kaug_ref.shape[0] // BLOCK
    kaug_ref[:, :LANES] = k_ref[0]
    kaug_ref[:, LANES:] = kf_ref[0]
    vaugt_ref[:LANES, :] = v_ref[0].astype(_F32).T.astype(_BF16)
    vaugt_ref[LANES:, :] = jnp.ones((SUM_ROWS, vaugt_ref.shape[1]), _BF16)
    shift = shift_ref[0, 0]
    min_sum = None
    for i in range(nb):
        sums = _attn_row_block(i, nb, shift, q_ref, f_ref, o_ref, kaug_ref, vaugt_ref)
        min_sum = sums if min_sum is None else jnp.minimum(min_sum, sums)
    healthy = jnp.min(min_sum) >= MIN_ROW_SUM

    @pl.when(jnp.logical_not(healthy))
    def _():
        vaug_ref[:, :LANES] = v_ref[0]
        vaug_ref[:, LANES:] = jnp.ones((vaug_ref.shape[0], LANES), _BF16)

        def row_block(i, carry):
            _attn_row_block_running_max(i, nb, q_ref, f_ref, o_ref, kaug_ref, vaug_ref,
                                        acc_ref, m_ref)
            return carry

        lax.fori_loop(0, nb, row_block, 0)


def _key_features(s):
    nb = s // BLOCK
    slopes = _alibi_slopes()
    pos = np.arange(s)
    feat = np.zeros((N_PAIRS, s, LANES), np.float32)
    onehot = (pos[:, None] // BLOCK == np.arange(nb)[None, :]).astype(np.float32)
    feat[:, :, 0:nb] = onehot
    feat[:, :, nb:2 * nb] = onehot
    for p in range(N_PAIRS):
        for slot in range(2):
            feat[p, :, 2 * nb + slot] = slopes[2 * p + slot] * (pos % BLOCK)
            feat[p, :, 2 * nb + 2 + slot] = 1.0
    assert np.array_equal(feat.astype(_BF16).astype(np.float32), feat)
    return jnp.asarray(feat, _BF16)


def _attention(q, f, k, v, shift):
    b, s, _ = q.shape
    kf = _key_features(s)
    seq = pl.BlockSpec((1, s, LANES), lambda bi, p: (bi, 0, p))
    return pl.pallas_call(
        _attn_kernel,
        grid=(b, N_PAIRS),
        in_specs=[pl.BlockSpec(memory_space=pltpu.SMEM), seq,
                  pl.BlockSpec((1, 1, s, LANES), lambda bi, p: (bi, p, 0, 0)),
                  seq, seq,
                  pl.BlockSpec((1, s, LANES), lambda bi, p: (p, 0, 0))],
        out_specs=seq,
        out_shape=jax.ShapeDtypeStruct((b, s, ATTN_WIDTH), _BF16),
        scratch_shapes=[pltpu.VMEM((s, 2 * LANES), _BF16),
                        pltpu.VMEM((LANES + SUM_ROWS, s), _BF16),
                        pltpu.VMEM((s, 2 * LANES), _BF16),
                        pltpu.VMEM((2 * BLOCK, 2 * LANES), _F32),
                        pltpu.VMEM((2 * BLOCK, LANES), _F32)],
        compiler_params=pltpu.CompilerParams(
            dimension_semantics=("arbitrary", "arbitrary"), vmem_limit_bytes=VMEM_LIMIT),
        name="moba_attention",
    )(shift, q, f, k, v, kf)


def kernel(x, ffn1_norm, ffn1_w1, ffn1_w3, ffn1_w2, mix_norm, w_in, q_norm, k_norm,
           conv_dw_w, conv_dw_b, conv_ln_g, conv_ln_b, w_out, ffn2_norm, ffn2_w1,
           ffn2_w3, ffn2_w2):
    b, s, d = x.shape
    assert d == D_MODEL and s % PROJ_ROWS == 0 and (b * s) % FFN_ROWS == 0
    bf = lambda w: w.astype(_BF16)
    row = lambda p: p.reshape(1, -1).astype(_F32)
    for l in range(ffn1_norm.shape[0]):
        x1 = _ffn(x.reshape(b * s, d), row(ffn1_norm[l]), bf(ffn1_w1[l]), bf(ffn1_w3[l]),
                  bf(ffn1_w2[l]))
        q, k, v, u, f = _in_proj(x1.reshape(b, s, d), row(mix_norm[l]), bf(w_in[l]),
                                 row(jnp.tile(q_norm[l], N_HEADS)),
                                 row(jnp.tile(k_norm[l], N_HEADS)))
        conv = _conv(u, conv_dw_w[l].astype(_F32), row(conv_dw_b[l]), row(conv_ln_g[l]),
                     row(conv_ln_b[l]))
        shift = (HEAD_DIM ** 0.5 * jnp.max(jnp.abs(q_norm[l])) * jnp.max(jnp.abs(k_norm[l]))
                 ).astype(_F32).reshape(1, 1)
        attn = _attention(q, f, k, v, shift)
        x = _ffn(x1, row(ffn2_norm[l]), bf(ffn2_w1[l]), bf(ffn2_w3[l]), bf(ffn2_w2[l]),
                 mix=(attn.reshape(b * s, ATTN_WIDTH), conv.reshape(b * s, CONV_WIDTH),
                      bf(w_out[l]))).reshape(b, s, d)
    return x
```
